```python
import jax
import jax.numpy as jnp
from jax import lax
import numpy as np

D_MODEL = 2048
BATCH = 4
SEQ = 4096
DEPTH = 1
DEC_BATCH = 8
DEC_SEQ = 2048
PAST_LEN = 128

D_MIX = D_MODEL
D_MLSTM = D_MIX // 2
D_RET = D_MIX - D_MLSTM
H_MLSTM = 4
H_RET = 4
DH_MLSTM = D_MLSTM // H_MLSTM
DH_RET = D_RET // H_RET
CHUNK = 128
D_FF = 5632
CONV_W = 3
N_IN = 4 * D_MLSTM + 4 * H_MLSTM + 4 * D_RET
DEEPNORM_ALPHA = (2.0 * DEPTH) ** 0.25
DEEPNORM_BETA = (8.0 * DEPTH) ** -0.25
LN_EPS = 1e-5
NORM_EPS = 1e-6
ROPE_BASE = 10000.0
RET_DECAY_EXP_FWD = 5.0
RET_DECAY_EXP_BWD = 5.5
NEG_INIT = -1e30

kernel_name = 'hymba_mlstm_retention_convffn_encoder'


def _layernorm(x, g, b):
    xf = x.astype(jnp.float32)
    mu = jnp.mean(xf, -1, keepdims=True)
    var = jnp.mean(jnp.square(xf - mu), -1, keepdims=True)
    y = (xf - mu) * lax.rsqrt(var + LN_EPS) * g.astype(jnp.float32) + b.astype(jnp.float32)
    return y.astype(x.dtype)


def _split_proj(z):
    sizes = [D_MLSTM] * 4 + [2 * H_MLSTM] * 2 + [D_RET] * 4
    return jnp.split(z, [int(s) for s in np.cumsum(sizes)[:-1]], axis=-1)


def _heads(t, n_heads):
    b, s, _ = t.shape
    return t.reshape(b, s, n_heads, -1).transpose(0, 2, 1, 3)


def _unheads(t):
    b, h, s, d = t.shape
    return t.transpose(0, 2, 1, 3).reshape(b, s, h * d)


def _two_dir(t_fwd, t_bwd):
    return jnp.concatenate([t_fwd, jnp.flip(t_bwd, axis=2)], axis=1)


def _merge_dir(y, n_heads):
    return y[:, :n_heads] + jnp.flip(y[:, n_heads:], axis=2)


def _to_chunks(t):
    b, h, s = t.shape[:3]
    t = t.reshape(b, h, s // CHUNK, CHUNK, *t.shape[3:])
    return jnp.moveaxis(t, 2, 0)


def _from_chunks(t):
    t = jnp.moveaxis(t, 0, 2)
    b, h, nc, l = t.shape[:4]
    return t.reshape(b, h, nc * l, *t.shape[4:])


def _rotary(t):
    s, d = t.shape[2], t.shape[3]
    inv = 1.0 / (ROPE_BASE ** jnp.linspace(0.0, 1.0, d // 2, dtype=jnp.float32))
    ang = jnp.arange(s, dtype=jnp.float32)[:, None] * inv[None, :]
    cos, sin = jnp.cos(ang), jnp.sin(ang)
    t2 = t.reshape(*t.shape[:-1], d // 2, 2)
    a, b = t2[..., 0], t2[..., 1]
    return jnp.stack([a * cos - b * sin, a * sin + b * cos], axis=-1).reshape(t.shape)


def _mlstm_chunkwise(q, k, v, ig, lf):
    b, h, s, dk = q.shape
    dv = v.shape[-1]
    causal = jnp.tril(jnp.ones((CHUNK, CHUNK), dtype=bool))

    def step(carry, inp):
        c_st, n_st, m_st = carry
        qj, kj, vj, ij, fj = inp
        g = jnp.cumsum(fj, axis=-1)
        log_d = g[..., :, None] - g[..., None, :] + ij[..., None, :]
        log_d = jnp.where(causal, log_d, -jnp.inf)
        inter = g + m_st[..., None]
        m_out = jnp.maximum(jnp.max(log_d, axis=-1), inter)
        d_mat = jnp.exp(log_d - m_out[..., None])
        inter_w = jnp.exp(inter - m_out)
        sc = jnp.einsum('bhld,bhsd->bhls', qj, kj) * d_mat
        num = jnp.einsum('bhls,bhse->bhle', sc, vj) + inter_w[..., None] * jnp.einsum('bhld,bhde->bhle', qj, c_st)
        den = jnp.sum(sc, axis=-1) + inter_w * jnp.einsum('bhld,bhd->bhl', qj, n_st)
        h_out = num / jnp.maximum(jnp.abs(den), jnp.exp(-m_out))[..., None]
        g_last = g[..., -1]
        w_log = g_last[..., None] - g + ij
        m_new = jnp.maximum(g_last + m_st, jnp.max(w_log, axis=-1))
        decay = jnp.exp(g_last + m_st - m_new)
        ws = jnp.exp(w_log - m_new[..., None])[..., None]
        c_new = decay[..., None, None] * c_st + jnp.einsum('bhsd,bhse->bhde', kj * ws, vj)
        n_new = decay[..., None] * n_st + jnp.sum(kj * ws, axis=2)
        return (c_new, n_new, m_new), h_out

    init = (jnp.zeros((b, h, dk, dv), jnp.float32), jnp.zeros((b, h, dk), jnp.float32),
            jnp.full((b, h), NEG_INIT, jnp.float32))
    _, hs = lax.scan(step, init, tuple(_to_chunks(t) for t in (q, k, v, ig, lf)))
    return _from_chunks(hs)


def _retention_chunkwise(q, k, v, log_gamma):
    b, h, s, dk = q.shape
    dv = v.shape[-1]
    pos = jnp.arange(CHUNK, dtype=jnp.float32)
    diff = pos[:, None] - pos[None, :]
    d_mat = jnp.where(diff >= 0, jnp.exp(log_gamma[:, None, None] * jnp.maximum(diff, 0.0)), 0.0)
    xi = jnp.exp(log_gamma[:, None] * (pos + 1.0))
    zeta = jnp.exp(log_gamma[:, None] * (CHUNK - 1.0 - pos))
    chunk_decay = jnp.exp(log_gamma * CHUNK)

    def step(r_st, inp):
        qj, kj, vj = inp
        sc = jnp.einsum('bhld,bhsd->bhls', qj, kj) * d_mat
        y = jnp.einsum('bhls,bhse->bhle', sc, vj) + xi[..., None] * jnp.einsum('bhld,bhde->bhle', qj, r_st)
        r_new = chunk_decay[:, None, None] * r_st + jnp.einsum('bhsd,bhse->bhde', kj * zeta[..., None], vj)
        return r_new, y

    _, ys = lax.scan(step, jnp.zeros((b, h, dk, dv), jnp.float32), tuple(_to_chunks(t) for t in (q, k, v)))
    return _from_chunks(ys)


def _ret_log_gamma():
    hd = jnp.arange(H_RET, dtype=jnp.float32)
    fwd = jnp.log1p(-jnp.exp2(-RET_DECAY_EXP_FWD - hd))
    bwd = jnp.log1p(-jnp.exp2(-RET_DECAY_EXP_BWD - hd))
    return jnp.concatenate([fwd, bwd])


def _mixer(h, w_in, b_igate, b_fgate, mlstm_norm_w, w_out):
    z = jnp.einsum('bsd,dn->bsn', h, w_in).astype(jnp.float32)
    mq, mk, mv, mo, mi, mf, rq, rk, rv, rg = _split_proj(z)
    q = _heads(mq, H_MLSTM) * DH_MLSTM ** -0.5
    k = _heads(mk, H_MLSTM)
    v = _heads(mv, H_MLSTM)
    ig = jnp.swapaxes(mi + b_igate.astype(jnp.float32), 1, 2)
    lf = jnp.swapaxes(jax.nn.log_sigmoid(mf + b_fgate.astype(jnp.float32)), 1, 2)
    hm = _mlstm_chunkwise(_two_dir(q, q), _two_dir(k, k), _two_dir(v, v),
                          _two_dir(ig[:, :H_MLSTM], ig[:, H_MLSTM:]),
                          _two_dir(lf[:, :H_MLSTM], lf[:, H_MLSTM:]))
    hm = _merge_dir(hm, H_MLSTM)
    mu = jnp.mean(hm, -1, keepdims=True)
    var = jnp.mean(jnp.square(hm - mu), -1, keepdims=True)
    hm = (hm - mu) * lax.rsqrt(var + NORM_EPS) * mlstm_norm_w.astype(jnp.float32).reshape(H_MLSTM, 1, DH_MLSTM)
    y_m = _unheads(hm) * jax.nn.sigmoid(mo)
    rqh = _rotary(_heads(rq, H_RET))
    rkh = _rotary(_heads(rk, H_RET)) * DH_RET ** -0.5
    rvh = _heads(rv, H_RET)
    yr = _retention_chunkwise(_two_dir(rqh, rqh), _two_dir(rkh, rkh), _two_dir(rvh, rvh), _ret_log_gamma())
    yr = _merge_dir(yr, H_RET)
    yr = yr * lax.rsqrt(jnp.mean(jnp.square(yr), -1, keepdims=True) + NORM_EPS)
    y_r = _unheads(yr) * jax.nn.silu(rg)
    y = jnp.concatenate([y_m, y_r], axis=-1).astype(h.dtype)
    return jnp.einsum('bsm,md->bsd', y, w_out)


def _conv_ffn(h, w_up, conv_w, conv_b, w_down):
    u = jnp.einsum('bsd,df->bsf', h, w_up)
    up = jnp.pad(u, ((0, 0), (1, 1), (0, 0)))
    u = up[:, :-2] * conv_w[0] + up[:, 1:-1] * conv_w[1] + up[:, 2:] * conv_w[2] + conv_b
    a, g = jnp.split(u, 2, axis=-1)
    return jnp.einsum('bsf,fd->bsd', jax.nn.silu(a) * g, w_down)


def _trunk(x, c, w_ada, b_ada, w_in, b_igate, b_fgate, mlstm_norm_w, w_out,
           ln1_g, ln1_b, w_up, conv_w, conv_b, w_down, ln2_g, ln2_b):
    for l in range(DEPTH):
        ada = jnp.einsum('bd,de->be', jax.nn.silu(c), w_ada[l]) + b_ada[l]
        sh1, sc1, g1, sh2, sc2, g2 = jnp.split(ada[:, None, :], 6, axis=-1)
        h = x * (1.0 + sc1) + sh1
        x = _layernorm(DEEPNORM_ALPHA * x + g1 * _mixer(h, w_in[l], b_igate[l], b_fgate[l], mlstm_norm_w[l], w_out[l]),
                       ln1_g[l], ln1_b[l])
        h = x * (1.0 + sc2) + sh2
        x = _layernorm(DEEPNORM_ALPHA * x + g2 * _conv_ffn(h, w_up[l], conv_w[l], conv_b[l], w_down[l]),
                       ln2_g[l], ln2_b[l])
    return x


def setup_inputs(seed: int = 0) -> dict:
    key = jax.random.key(seed)
    ks = jax.random.split(key, 20)

    def nrm(k, shape, s):
        return jax.random.normal(k, shape, jnp.float32) * s

    L = DEPTH
    f_base = jnp.tile(jnp.linspace(3.0, 6.0, H_MLSTM, dtype=jnp.float32), 2)
    return {
        'x_prompt': nrm(ks[0], (BATCH, SEQ, D_MODEL), 1.0),
        'x_sample': nrm(ks[1], (DEC_BATCH, DEC_SEQ, D_MODEL), 1.0),
        'c_prompt': nrm(ks[2], (BATCH, D_MODEL), 1.0),
        'c_sample': nrm(ks[3], (DEC_BATCH, D_MODEL), 1.0),
        'w_ada': nrm(ks[4], (L, D_MODEL, 6 * D_MODEL), 0.5 * D_MODEL ** -0.5),
        'b_ada': nrm(ks[5], (L, 6 * D_MODEL), 0.02),
        'w_in': nrm(ks[6], (L, D_MODEL, N_IN), D_MODEL ** -0.5),
        'b_igate': nrm(ks[7], (L, 2 * H_MLSTM), 0.1),
        'b_fgate': f_base + nrm(ks[8], (L, 2 * H_MLSTM), 0.1),
        'mlstm_norm_w': 1.0 + nrm(ks[9], (L, D_MLSTM), 0.02),
        'w_out': nrm(ks[10], (L, D_MIX, D_MODEL), DEEPNORM_BETA * D_MIX ** -0.5),
        'ln1_g': 1.0 + nrm(ks[11], (L, D_MODEL), 0.02),
        'ln1_b': nrm(ks[12], (L, D_MODEL), 0.02),
        'w_up': nrm(ks[13], (L, D_MODEL, 2 * D_FF), D_MODEL ** -0.5),
        'conv_w': nrm(ks[14], (L, CONV_W, 2 * D_FF), CONV_W ** -0.5),
        'conv_b': nrm(ks[15], (L, 2 * D_FF), 0.02),
        'w_down': nrm(ks[16], (L, D_FF, D_MODEL), DEEPNORM_BETA * D_FF ** -0.5),
        'ln2_g': 1.0 + nrm(ks[17], (L, D_MODEL), 0.02),
        'ln2_b': nrm(ks[18], (L, D_MODEL), 0.02),
    }


def reference(x_prompt, x_sample, c_prompt, c_sample, w_ada, b_ada, w_in, b_igate, b_fgate,
              mlstm_norm_w, w_out, ln1_g, ln1_b, w_up, conv_w, conv_b, w_down, ln2_g, ln2_b):
    y_prompt = _trunk(x_prompt, c_prompt, w_ada, b_ada, w_in, b_igate, b_fgate, mlstm_norm_w, w_out,
                      ln1_g, ln1_b, w_up, conv_w, conv_b, w_down, ln2_g, ln2_b)
    y_sample = _trunk(x_sample, c_sample, w_ada, b_ada, w_in, b_igate, b_fgate, mlstm_norm_w, w_out,
                      ln1_g, ln1_b, w_up, conv_w, conv_b, w_down, ln2_g, ln2_b)
    return (y_prompt, y_sample)
```

```python
import functools

import jax
import jax.numpy as jnp
import numpy as np
from jax import lax
from jax.experimental import pallas as pl
from jax.experimental.pallas import tpu as pltpu

F32 = jnp.float32
BF16 = jnp.bfloat16

D_MODEL = 2048
N_HEADS = 4
D_HEAD = 256
D_GROUP = N_HEADS * D_HEAD
CHUNK = 128
D_FF = 5632
DEPTH = 1
DEEPNORM_ALPHA = (2.0 * DEPTH) ** 0.25
LN_EPS = 1e-5
NORM_EPS = 1e-6
ROPE_BASE = 10000.0
RET_DECAY_EXP_FWD = 5.0
RET_DECAY_EXP_BWD = 5.5
NEG_INIT = -1e30
HEAD_SCALE = D_HEAD ** -0.5

SUBLANES = 8
LANES = 128
VMEM_LIMIT_BYTES = 56 * 1024 * 1024

ZB_MQ, ZB_MK, ZB_MV, ZB_MO, ZB_RQ, ZB_RK, ZB_RV, ZB_RG = (N_HEADS * i for i in range(8))
N_ZBLOCKS = 8 * N_HEADS

PK_M, PK_E, PK_IW, PK_WS, PK_DEC = 0, 2, 4, 6, 8
N_PACK_ROWS = 10


def _sigmoid(x):
    return 1.0 / (1.0 + jnp.exp(-x))


def _log_sigmoid(x):
    return jnp.minimum(x, 0.0) - jnp.log1p(jnp.exp(-jnp.abs(x)))


def _layernorm(v, g, b):
    mu = jnp.mean(v, axis=-1, keepdims=True)
    d = v - mu
    var = jnp.mean(d * d, axis=-1, keepdims=True)
    return d * lax.rsqrt(var + LN_EPS) * g + b


def _params(*semantics):
    return pltpu.CompilerParams(dimension_semantics=semantics, vmem_limit_bytes=VMEM_LIMIT_BYTES)


ADA_TN = 1024


def _ada_kernel(c_ref, w_ref, b_ref, o_ref):
    c = c_ref[...]
    s = (c * _sigmoid(c)).astype(BF16)
    o_ref[...] = jnp.dot(s, w_ref[...].astype(BF16), preferred_element_type=F32) + b_ref[...]


def _ada(c, w_ada, b_ada):
    rows = c.shape[0]
    n_out = w_ada.shape[1]
    return pl.pallas_call(
        _ada_kernel,
        out_shape=jax.ShapeDtypeStruct((rows, n_out), F32),
        grid=(n_out // ADA_TN,),
        in_specs=[
            pl.BlockSpec((rows, D_MODEL), lambda n: (0, 0)),
            pl.BlockSpec((D_MODEL, ADA_TN), lambda n: (0, n)),
            pl.BlockSpec((1, ADA_TN), lambda n: (0, n)),
        ],
        out_specs=pl.BlockSpec((rows, ADA_TN), lambda n: (0, n)),
        compiler_params=_params("arbitrary"),
        name="ada",
    )(c, w_ada, b_ada)


INPROJ_TM = 1024
INPROJ_TN = D_GROUP
CHUNKS_PER_TM = INPROJ_TM // CHUNK


def _inproj_kernel(x_ref, sc_ref, sh_ref, w_ref, wgi_ref, wgf_ref, cos_ref, sin_ref,
                   z_ref, gi_ref, gf_ref, h_ref):
    n = pl.program_id(1)

    @pl.when(n == 0)
    def _():
        hb = (x_ref[...] * (1.0 + sc_ref[...]) + sh_ref[...]).astype(BF16)
        h_ref[...] = hb
        nt = (((1,), (1,)), ((), ()))
        gi = lax.dot_general(wgi_ref[...], hb, nt, preferred_element_type=F32)
        gf = lax.dot_general(wgf_ref[...], hb, nt, preferred_element_type=F32)
        for c in range(CHUNKS_PER_TM):
            gi_ref[c * SUBLANES:(c + 1) * SUBLANES, :] = gi[:, c * CHUNK:(c + 1) * CHUNK]
            gf_ref[c * SUBLANES:(c + 1) * SUBLANES, :] = gf[:, c * CHUNK:(c + 1) * CHUNK]

    res = jnp.dot(h_ref[...], w_ref[...], preferred_element_type=F32)

    def store(fn):
        for c in range(N_HEADS):
            z_ref[c] = fn(res[:, c * D_HEAD:(c + 1) * D_HEAD]).astype(BF16)

    def store_rotary(scale):
        cos = cos_ref[...]
        sin = sin_ref[...]
        half = D_HEAD // 2
        for c in range(N_HEADS):
            a = res[:, c * D_HEAD:c * D_HEAD + half]
            b = res[:, c * D_HEAD + half:(c + 1) * D_HEAD]
            z_ref[c, :, 0:half] = ((a * cos - b * sin) * scale).astype(BF16)
            z_ref[c, :, half:D_HEAD] = ((a * sin + b * cos) * scale).astype(BF16)

    nb = lambda zb: zb // N_HEADS

    @pl.when(n == nb(ZB_MQ))
    def _():
        store(lambda t: t * HEAD_SCALE)

    @pl.when((n == nb(ZB_MK)) | (n == nb(ZB_MV)) | (n == nb(ZB_RV)))
    def _():
        store(lambda t: t)

    @pl.when(n == nb(ZB_MO))
    def _():
        store(_sigmoid)

    @pl.when(n == nb(ZB_RQ))
    def _():
        store_rotary(1.0)

    @pl.when(n == nb(ZB_RK))
    def _():
        store_rotary(HEAD_SCALE)

    @pl.when(n == nb(ZB_RG))
    def _():
        store(lambda t: t * _sigmoid(t))


def _in_proj(x2, sc, sh, w_main, wgi, wgf, cos, sin, seq):
    tokens = x2.shape[0]
    tm, tn = INPROJ_TM, INPROJ_TN
    tiles_per_seq = seq // tm
    batch_of = lambda i: i // tiles_per_seq
    return pl.pallas_call(
        _inproj_kernel,
        out_shape=(
            jax.ShapeDtypeStruct((N_ZBLOCKS, tokens, D_HEAD), BF16),
            jax.ShapeDtypeStruct((tokens // CHUNK * SUBLANES, CHUNK), F32),
            jax.ShapeDtypeStruct((tokens // CHUNK * SUBLANES, CHUNK), F32),
        ),
        grid=(tokens // tm, w_main.shape[1] // tn),
        in_specs=[
            pl.BlockSpec((tm, D_MODEL), lambda i, n: (i, 0)),
            pl.BlockSpec((None, 1, D_MODEL), lambda i, n: (batch_of(i), 0, 0)),
            pl.BlockSpec((None, 1, D_MODEL), lambda i, n: (batch_of(i), 0, 0)),
            pl.BlockSpec((D_MODEL, tn), lambda i, n: (0, n)),
            pl.BlockSpec((SUBLANES, D_MODEL), lambda i, n: (0, 0)),
            pl.BlockSpec((SUBLANES, D_MODEL), lambda i, n: (0, 0)),
            pl.BlockSpec((tm, D_HEAD // 2), lambda i, n: (i % tiles_per_seq, 0)),
            pl.BlockSpec((tm, D_HEAD // 2), lambda i, n: (i % tiles_per_seq, 0)),
        ],
        out_specs=(
            pl.BlockSpec((N_HEADS, tm, D_HEAD), lambda i, n: (n, i, 0)),
            pl.BlockSpec((CHUNKS_PER_TM * SUBLANES, CHUNK), lambda i, n: (i, 0)),
            pl.BlockSpec((CHUNKS_PER_TM * SUBLANES, CHUNK), lambda i, n: (i, 0)),
        ),
        scratch_shapes=[pltpu.VMEM((tm, D_MODEL), BF16)],
        compiler_params=_params("arbitrary", "arbitrary"),
        name="in_proj",
    )(x2, sc, sh, w_main, wgi, wgf, cos, sin)


def _lane_scan(x, op, suffix, lane):
    k = 1
    while k < CHUNK:
        if suffix:
            shifted = pltpu.roll(x, CHUNK - k, axis=1)
            valid = lane < CHUNK - k
        else:
            shifted = pltpu.roll(x, k, axis=1)
            valid = lane >= k
        x = jnp.where(valid, op(x, shifted), x)
        k *= 2
    return x


def _lane_allreduce(x, op):
    k = 1
    while k < CHUNK:
        x = op(x, pltpu.roll(x, k, axis=1))
        k *= 2
    return x


def _gates_kernel(gi_ref, gf_ref, bi_ref, bf_ref, pack_ref, arow_ref,
                  a_s, g_s, cm_s, amax_s, glast_s, mstf_s, mstb_s, *, n_chunks):
    rows = n_chunks * SUBLANES
    lane = lax.broadcasted_iota(jnp.int32, (rows, CHUNK), 1)
    is_bwd = (lax.broadcasted_iota(jnp.int32, (rows, CHUNK), 0) % SUBLANES) >= N_HEADS

    ig = gi_ref[...] + bi_ref[...]
    lf = _log_sigmoid(gf_ref[...] + bf_ref[...])
    g = jnp.where(is_bwd, _lane_scan(lf, jnp.add, True, lane), _lane_scan(lf, jnp.add, False, lane))
    a = ig - g
    cm = jnp.where(is_bwd, _lane_scan(a, jnp.maximum, True, lane), _lane_scan(a, jnp.maximum, False, lane))
    a_s[...] = a
    g_s[...] = g
    cm_s[...] = cm
    arow_ref[...] = a
    amax_s[...] = _lane_allreduce(a, jnp.maximum)
    glast_s[...] = _lane_allreduce(lf, jnp.add)

    def rec(i, carry):
        mf, mb = carry
        rf = pl.ds(pl.multiple_of(i * SUBLANES, SUBLANES), SUBLANES)
        rb = pl.ds(pl.multiple_of((n_chunks - 1 - i) * SUBLANES, SUBLANES), SUBLANES)
        mstf_s[rf, :] = mf
        mstb_s[rb, :] = mb
        mf = glast_s[rf, :] + jnp.maximum(mf, amax_s[rf, :])
        mb = glast_s[rb, :] + jnp.maximum(mb, amax_s[rb, :])
        return mf, mb

    init = jnp.full((SUBLANES, CHUNK), NEG_INIT, F32)
    lax.fori_loop(0, n_chunks, rec, (init, init))

    mst = jnp.where(is_bwd, mstb_s[...], mstf_s[...])
    m_row = jnp.maximum(cm_s[...], mst)
    m_last = jnp.maximum(amax_s[...], mst)
    e = jnp.exp(-(g_s[...] + m_row))
    iw = jnp.exp(mst - m_row)
    ws = jnp.exp(a_s[...] - m_last)
    dec = jnp.exp(mst - m_last)
    cm_s[...] = m_row
    g_s[...] = e
    a_s[...] = iw
    amax_s[...] = ws
    glast_s[...] = dec

    top_rows = 2 * SUBLANES
    assert N_PACK_ROWS <= top_rows
    top_row = lax.broadcasted_iota(jnp.int32, (top_rows, CHUNK), 0)
    pad = jnp.zeros((CHUNK - top_rows, CHUNK), F32)

    def emit(j, _):
        r0 = pl.multiple_of(j * SUBLANES, SUBLANES)
        tok = pl.ds(pl.multiple_of(j * CHUNK, CHUNK), CHUNK)
        quantities = [s[pl.ds(r0, SUBLANES), :] for s in (cm_s, g_s, a_s, amax_s, glast_s)]
        for h in range(N_HEADS):
            top = jnp.zeros((top_rows, CHUNK), F32)
            for qi, qt in enumerate(quantities):
                for direction in range(2):
                    src = N_HEADS * direction + h
                    top = jnp.where(top_row == 2 * qi + direction, qt[src:src + 1, :], top)
            tile = jnp.concatenate([top, pad], axis=0)
            pack_ref[h, tok, :] = tile.T
        return 0

    lax.fori_loop(0, n_chunks, emit, 0)


def _gates(gi, gf, bi_col, bf_col, batch, seq):
    n_chunks = seq // CHUNK
    rows = n_chunks * SUBLANES
    scratch = [pltpu.VMEM((rows, CHUNK), F32) for _ in range(7)]
    return pl.pallas_call(
        functools.partial(_gates_kernel, n_chunks=n_chunks),
        out_shape=(
            jax.ShapeDtypeStruct((batch, N_HEADS, seq, LANES), F32),
            jax.ShapeDtypeStruct((batch * rows, CHUNK), F32),
        ),
        grid=(batch,),
        in_specs=[
            pl.BlockSpec((rows, CHUNK), lambda b: (b, 0)),
            pl.BlockSpec((rows, CHUNK), lambda b: (b, 0)),
            pl.BlockSpec((rows, CHUNK), lambda b: (0, 0)),
            pl.BlockSpec((rows, CHUNK), lambda b: (0, 0)),
        ],
        out_specs=(
            pl.BlockSpec((None, N_HEADS, seq, LANES), lambda b: (b, 0, 0, 0)),
            pl.BlockSpec((rows, CHUNK), lambda b: (b, 0)),
        ),
        scratch_shapes=scratch,
        compiler_params=_params("arbitrary"),
        name="gates",
    )(gi, gf, bi_col, bf_col)


_NT = (((1,), (1,)), ((), ()))
_TN = (((0,), (0,)), ((), ()))


def _causal_mask(direction):
    row = lax.broadcasted_iota(jnp.int32, (CHUNK, CHUNK), 0)
    col = lax.broadcasted_iota(jnp.int32, (CHUNK, CHUNK), 1)
    return (col <= row) if direction == 0 else (col >= row)


def _mlstm_kernel(q_ref, k_ref, v_ref, o_ref, pack_ref, arow_ref, nw_ref, y_ref,
                  hacc_s, c_s, n_s, *, n_chunks):
    head = pl.program_id(1)

    def step(direction, j):
        tok = pl.ds(pl.multiple_of(j * CHUNK, CHUNK), CHUNK)
        q = q_ref[tok, :]
        k = k_ref[tok, :]
        v = v_ref[tok, :]
        pk = pack_ref[tok, :]
        col = lambda base: pk[:, base + direction:base + direction + 1]
        m_row, e, iw, ws = col(PK_M), col(PK_E), col(PK_IW), col(PK_WS)
        dec = pk[0:1, PK_DEC + direction:PK_DEC + direction + 1]
        a = arow_ref[pl.ds(j * SUBLANES + head + N_HEADS * direction, 1), :]

        s = lax.dot_general(q, k, _NT, preferred_element_type=F32)
        sc = s * jnp.where(_causal_mask(direction), jnp.exp(a - m_row), 0.0)
        c_st = c_s[direction]
        n_st = n_s[direction]
        inter = jnp.dot(q, c_st.astype(BF16), preferred_element_type=F32)
        num = jnp.dot(sc.astype(BF16), v, preferred_element_type=F32) + iw * inter
        qn = jnp.sum(q.astype(F32) * n_st, axis=1, keepdims=True)
        den = jnp.sum(sc, axis=1, keepdims=True) + iw * qn
        h_out = num * (1.0 / jnp.maximum(jnp.abs(den), e))

        kw = k.astype(F32) * ws
        kv = lax.dot_general(kw.astype(BF16), v, _TN, preferred_element_type=F32)
        c_s[direction] = dec * c_st + kv
        n_s[direction] = dec * n_st + jnp.sum(kw, axis=0, keepdims=True)
        return tok, h_out

    c_s[...] = jnp.zeros_like(c_s)
    n_s[...] = jnp.zeros_like(n_s)

    def fwd(i, _):
        tok, h_out = step(0, i)
        hacc_s[tok, :] = h_out
        return 0

    lax.fori_loop(0, n_chunks, fwd, 0)

    def bwd(i, _):
        tok, h_out = step(1, n_chunks - 1 - i)
        hm = hacc_s[tok, :] + h_out
        mu = jnp.mean(hm, axis=-1, keepdims=True)
        d = hm - mu
        var = jnp.mean(d * d, axis=-1, keepdims=True)
        y = d * lax.rsqrt(var + NORM_EPS) * nw_ref[...] * o_ref[tok, :].astype(F32)
        y_ref[tok, :] = y.astype(BF16)
        return 0

    lax.fori_loop(0, n_chunks, bwd, 0)


def _mlstm(z3, pack, arow, norm_w, batch, seq):
    n_chunks = seq // CHUNK
    zspec = lambda zb: pl.BlockSpec((None, seq, D_HEAD), lambda b, h: (zb + h, b, 0))
    return pl.pallas_call(
        functools.partial(_mlstm_kernel, n_chunks=n_chunks),
        out_shape=jax.ShapeDtypeStruct((N_HEADS, batch * seq, D_HEAD), BF16),
        grid=(batch, N_HEADS),
        in_specs=[
            zspec(ZB_MQ), zspec(ZB_MK), zspec(ZB_MV), zspec(ZB_MO),
            pl.BlockSpec((None, None, seq, LANES), lambda b, h: (b, h, 0, 0)),
            pl.BlockSpec((n_chunks * SUBLANES, CHUNK), lambda b, h: (b, 0)),
            pl.BlockSpec((None, 1, D_HEAD), lambda b, h: (h, 0, 0)),
        ],
        out_specs=pl.BlockSpec((None, seq, D_HEAD), lambda b, h: (h, b, 0)),
        scratch_shapes=[
            pltpu.VMEM((seq, D_HEAD), F32),
            pltpu.VMEM((2, D_HEAD, D_HEAD), F32),
            pltpu.VMEM((2, 1, D_HEAD), F32),
        ],
        compiler_params=_params("arbitrary", "arbitrary"),
        name="mlstm",
    )(z3, z3, z3, z3, pack, arow, norm_w)


def _ret_kernel(q_ref, k_ref, v_ref, g_ref, dmat_ref, xi_ref, zeta_ref, cd_ref, y_ref,
                yacc_s, r_s, *, n_chunks):
    def step(direction, j):
        tok = pl.ds(pl.multiple_of(j * CHUNK, CHUNK), CHUNK)
        q = q_ref[tok, :]
        k = k_ref[tok, :]
        v = v_ref[tok, :]
        s = lax.dot_general(q, k, _NT, preferred_element_type=F32) * dmat_ref[direction]
        r_st = r_s[direction]
        inter = jnp.dot(q, r_st.astype(BF16), preferred_element_type=F32)
        y = jnp.dot(s.astype(BF16), v, preferred_element_type=F32) + xi_ref[direction] * inter
        kz = (k.astype(F32) * zeta_ref[direction]).astype(BF16)
        kv = lax.dot_general(kz, v, _TN, preferred_element_type=F32)
        r_s[direction] = cd_ref[direction][:, 0:1] * r_st + kv
        return tok, y

    r_s[...] = jnp.zeros_like(r_s)

    def fwd(i, _):
        tok, y = step(0, i)
        yacc_s[tok, :] = y
        return 0

    lax.fori_loop(0, n_chunks, fwd, 0)

    def bwd(i, _):
        tok, y = step(1, n_chunks - 1 - i)
        yr = yacc_s[tok, :] + y
        yr = yr * lax.rsqrt(jnp.mean(yr * yr, axis=-1, keepdims=True) + NORM_EPS)
        y_ref[tok, :] = (yr * g_ref[tok, :].astype(F32)).astype(BF16)
        return 0

    lax.fori_loop(0, n_chunks, bwd, 0)


def _retention(z3, dmat, xi, zeta, cd, batch, seq):
    n_chunks = seq // CHUNK
    zspec = lambda zb: pl.BlockSpec((None, seq, D_HEAD), lambda b, h: (zb + h, b, 0))
    per_head = lambda *tail: pl.BlockSpec((None, 2) + tail, lambda b, h: (h, 0) + (0,) * len(tail))
    return pl.pallas_call(
        functools.partial(_ret_kernel, n_chunks=n_chunks),
        out_shape=jax.ShapeDtypeStruct((N_HEADS, batch * seq, D_HEAD), BF16),
        grid=(batch, N_HEADS),
        in_specs=[
            zspec(ZB_RQ), zspec(ZB_RK), zspec(ZB_RV), zspec(ZB_RG),
            per_head(CHUNK, CHUNK), per_head(CHUNK, D_HEAD), per_head(CHUNK, D_HEAD), per_head(1, LANES),
        ],
        out_specs=pl.BlockSpec((None, seq, D_HEAD), lambda b, h: (h, b, 0)),
        scratch_shapes=[
            pltpu.VMEM((seq, D_HEAD), F32),
            pltpu.VMEM((2, D_HEAD, D_HEAD), F32),
        ],
        compiler_params=_params("arbitrary", "arbitrary"),
        name="retention",
    )(z3, z3, z3, z3, dmat, xi, zeta, cd)


OUTPROJ_TM = 512


def _outproj_kernel(ym_ref, yr_ref, x_ref, g_ref, w_ref, lg_ref, lb_ref, o_ref):
    acc = None
    for grp, y_ref in enumerate((ym_ref, yr_ref)):
        for c in range(N_HEADS):
            r0 = grp * D_GROUP + c * D_HEAD
            part = jnp.dot(y_ref[c], w_ref[r0:r0 + D_HEAD, :], preferred_element_type=F32)
            acc = part if acc is None else acc + part
    v = DEEPNORM_ALPHA * x_ref[...] + g_ref[...] * acc
    o_ref[...] = _layernorm(v, lg_ref[...], lb_ref[...])


def _out_proj(ym, yr, x2, g1, w_out, ln_g, ln_b, seq):
    tokens = x2.shape[0]
    tm = OUTPROJ_TM
    tiles_per_seq = seq // tm
    row = pl.BlockSpec((1, D_MODEL), lambda i: (0, 0))
    return pl.pallas_call(
        _outproj_kernel,
        out_shape=jax.ShapeDtypeStruct((tokens, D_MODEL), F32),
        grid=(tokens // tm,),
        in_specs=[
            pl.BlockSpec((N_HEADS, tm, D_HEAD), lambda i: (0, i, 0)),
            pl.BlockSpec((N_HEADS, tm, D_HEAD), lambda i: (0, i, 0)),
            pl.BlockSpec((tm, D_MODEL), lambda i: (i, 0)),
            pl.BlockSpec((None, 1, D_MODEL), lambda i: (i // tiles_per_seq, 0, 0)),
            pl.BlockSpec((D_MODEL, D_MODEL), lambda i: (0, 0)),
            row, row,
        ],
        out_specs=pl.BlockSpec((tm, D_MODEL), lambda i: (i, 0)),
        compiler_params=_params("arbitrary"),
        name="out_proj",
    )(ym, yr, x2, g1, w_out, ln_g, ln_b)


FFN_TM = 512
FFN_TF = 512
HALO = SUBLANES


def _ffn_kernel(x_ref, xp_ref, xn_ref, sc_ref, sh_ref, g_ref, wa_ref, wg_ref, cwa_ref, cwg_ref,
                cba_ref, cbg_ref, wd_ref, lg_ref, lb_ref, o_ref, h_s, acc_s, *, tiles_per_seq):
    i = pl.program_id(0)
    f = pl.program_id(1)
    tm = FFN_TM

    @pl.when(f == 0)
    def _():
        scale = 1.0 + sc_ref[...]
        shift = sh_ref[...]
        has_prev = (i % tiles_per_seq != 0).astype(F32)
        has_next = (i % tiles_per_seq != tiles_per_seq - 1).astype(F32)
        h_s[0:HALO, :] = ((xp_ref[...] * scale + shift) * has_prev).astype(BF16)
        h_s[HALO:HALO + tm, :] = (x_ref[...] * scale + shift).astype(BF16)
        h_s[HALO + tm:, :] = ((xn_ref[...] * scale + shift) * has_next).astype(BF16)
        acc_s[...] = jnp.zeros_like(acc_s)

    h = h_s[...]

    def conv_branch(w_ref, cw_ref, cb_ref):
        u = jnp.dot(h, w_ref[...], preferred_element_type=F32)
        cw = cw_ref[...]
        return (cw[0:1, :] * u[HALO - 1:HALO - 1 + tm, :] + cw[1:2, :] * u[HALO:HALO + tm, :]
                + cw[2:3, :] * u[HALO + 1:HALO + 1 + tm, :] + cb_ref[...])

    a = conv_branch(wa_ref, cwa_ref, cba_ref)
    g = conv_branch(wg_ref, cwg_ref, cbg_ref)
    act = (a * _sigmoid(a) * g).astype(BF16)
    acc_s[...] += jnp.dot(act, wd_ref[...], preferred_element_type=F32)

    @pl.when(f == pl.num_programs(1) - 1)
    def _():
        v = DEEPNORM_ALPHA * x_ref[...] + g_ref[...] * acc_s[...]
        o_ref[...] = _layernorm(v, lg_ref[...], lb_ref[...])


def _ffn(x1, sc2, sh2, g2, w_up, conv_w, conv_b, w_down, ln_g, ln_b, seq):
    tokens = x1.shape[0]
    tm, tf = FFN_TM, FFN_TF
    tiles_per_seq = seq // tm
    n_f = D_FF // tf
    halo_per_tile = tm // HALO
    n_halo_blocks = tokens // HALO
    mod = pl.BlockSpec((None, 1, D_MODEL), lambda i, f: (i // tiles_per_seq, 0, 0))
    row = pl.BlockSpec((1, D_MODEL), lambda i, f: (0, 0))
    return pl.pallas_call(
        functools.partial(_ffn_kernel, tiles_per_seq=tiles_per_seq),
        out_shape=jax.ShapeDtypeStruct((tokens, D_MODEL), F32),
        grid=(tokens // tm, n_f),
        in_specs=[
            pl.BlockSpec((tm, D_MODEL), lambda i, f: (i, 0)),
            pl.BlockSpec((HALO, D_MODEL), lambda i, f: (jnp.maximum(i * halo_per_tile - 1, 0), 0)),
            pl.BlockSpec((HALO, D_MODEL),
                         lambda i, f: (jnp.minimum((i + 1) * halo_per_tile, n_halo_blocks - 1), 0)),
            mod, mod, mod,
            pl.BlockSpec((D_MODEL, tf), lambda i, f: (0, f)),
            pl.BlockSpec((D_MODEL, tf), lambda i, f: (0, n_f + f)),
            pl.BlockSpec((3, tf), lambda i, f: (0, f)),
            pl.BlockSpec((3, tf), lambda i, f: (0, n_f + f)),
            pl.BlockSpec((1, tf), lambda i, f: (0, f)),
            pl.BlockSpec((1, tf), lambda i, f: (0, n_f + f)),
            pl.BlockSpec((tf, D_MODEL), lambda i, f: (f, 0)),
            row, row,
        ],
        out_specs=pl.BlockSpec((tm, D_MODEL), lambda i, f: (i, 0)),
        scratch_shapes=[
            pltpu.VMEM((tm + 2 * HALO, D_MODEL), BF16),
            pltpu.VMEM((tm, D_MODEL), F32),
        ],
        compiler_params=_params("arbitrary", "arbitrary"),
        name="ffn",
    )(x1, x1, x1, sc2, sh2, g2, w_up, w_up, conv_w, conv_w, conv_b, conv_b, w_down, ln_g, ln_b)


def _rotary_tables(seq):
    half = D_HEAD // 2
    inv = 1.0 / (ROPE_BASE ** jnp.linspace(0.0, 1.0, half, dtype=F32))
    ang = jnp.arange(seq, dtype=F32)[:, None] * inv[None, :]
    return jnp.cos(ang), jnp.sin(ang)


def _retention_tables():
    hd = jnp.arange(N_HEADS, dtype=F32)
    lg = jnp.stack([jnp.log1p(-jnp.exp2(-RET_DECAY_EXP_FWD - hd)),
                    jnp.log1p(-jnp.exp2(-RET_DECAY_EXP_BWD - hd))], axis=1)
    pos = jnp.arange(CHUNK, dtype=F32)
    diff = pos[:, None] - pos[None, :]
    dmat = jnp.where(diff >= 0, jnp.exp(lg[:, :, None, None] * jnp.maximum(diff, 0.0)), 0.0)
    xi = jnp.exp(lg[:, :, None] * (pos + 1.0))
    zeta = jnp.exp(lg[:, :, None] * (CHUNK - 1.0 - pos))
    cd = jnp.exp(lg * CHUNK)
    dmat = jnp.stack([dmat[:, 0], dmat[:, 1, ::-1, ::-1]], axis=1)
    xi = jnp.stack([xi[:, 0], xi[:, 1, ::-1]], axis=1)
    zeta = jnp.stack([zeta[:, 0], zeta[:, 1, ::-1]], axis=1)
    bcast = lambda t: jnp.broadcast_to(t[..., None], t.shape + (D_HEAD,))
    cd = jnp.broadcast_to(cd[:, :, None, None], (N_HEADS, 2, 1, LANES))
    return dmat, bcast(xi), bcast(zeta), cd


def _split_pairs(w):
    w = w.reshape(D_MODEL, N_HEADS, D_HEAD // 2, 2)
    return jnp.concatenate([w[..., 0], w[..., 1]], axis=-1).reshape(D_MODEL, D_GROUP)


def _layout_w_in(w_in):
    sizes = [D_GROUP] * 4 + [2 * N_HEADS] * 2 + [D_GROUP] * 4
    mq, mk, mv, mo, mi, mf, rq, rk, rv, rg = jnp.split(w_in, [int(s) for s in np.cumsum(sizes)[:-1]], axis=1)
    w_main = jnp.concatenate([mq, mk, mv, mo, _split_pairs(rq), _split_pairs(rk), rv, rg], axis=1).astype(BF16)
    return w_main, mi.T.astype(BF16), mf.T.astype(BF16)


def _trunk(x, ada, weights, tables):
    batch, seq, _ = x.shape
    w_main, wgi, wgf, bi_col, bf_col, norm_w, w_out, ln1_g, ln1_b, w_up, conv_w, conv_b, w_down, ln2_g, ln2_b = weights
    dmat, xi, zeta, cd = tables
    sh1, sc1, g1, sh2, sc2, g2 = (t.reshape(batch, 1, D_MODEL) for t in jnp.split(ada, 6, axis=-1))
    x2 = x.reshape(batch * seq, D_MODEL)
    cos, sin = _rotary_tables(seq)
    n_rows = seq // CHUNK * SUBLANES

    z3, gi, gf = _in_proj(x2, sc1, sh1, w_main, wgi, wgf, cos, sin, seq)
    pack, arow = _gates(gi, gf, jnp.tile(bi_col, (n_rows // SUBLANES, 1)), jnp.tile(bf_col, (n_rows // SUBLANES, 1)),
                        batch, seq)
    ym = _mlstm(z3, pack, arow, norm_w, batch, seq)
    yr = _retention(z3, dmat, xi, zeta, cd, batch, seq)
    x1 = _out_proj(ym, yr, x2, g1, w_out, ln1_g, ln1_b, seq)
    out = _ffn(x1, sc2, sh2, g2, w_up, conv_w, conv_b, w_down, ln2_g, ln2_b, seq)
    return out.reshape(batch, seq, D_MODEL)


def kernel(x_prompt, x_sample, c_prompt, c_sample, w_ada, b_ada, w_in, b_igate, b_fgate, mlstm_norm_w, w_out,
           ln1_g, ln1_b, w_up, conv_w, conv_b, w_down, ln2_g, ln2_b):
    assert w_ada.shape[0] == DEPTH
    n_prompt, n_sample = c_prompt.shape[0], c_sample.shape[0]
    c_all = jnp.concatenate([c_prompt, c_sample], axis=0)
    pad_rows = -c_all.shape[0] % SUBLANES
    c_all = jnp.pad(c_all, ((0, pad_rows), (0, 0)))
    ada = _ada(c_all, w_ada[0], b_ada[0][None, :])

    w_main, wgi, wgf = _layout_w_in(w_in[0])
    bcast_col = lambda b: jnp.broadcast_to(b[:, None], (2 * N_HEADS, CHUNK))
    weights = (
        w_main, wgi, wgf, bcast_col(b_igate[0]), bcast_col(b_fgate[0]),
        mlstm_norm_w[0].reshape(N_HEADS, 1, D_HEAD), w_out[0].astype(BF16), ln1_g, ln1_b,
        w_up[0].astype(BF16), conv_w[0], conv_b, w_down[0].astype(BF16), ln2_g, ln2_b,
    )
    tables = _retention_tables()
    y_prompt = _trunk(x_prompt, ada[:n_prompt], weights, tables)
    y_sample = _trunk(x_sample, ada[n_prompt:n_prompt + n_sample], weights, tables)
    return (y_prompt, y_sample)
```

```python
import functools

import jax
import jax.numpy as jnp
import numpy as np
from jax import lax
from jax.experimental import pallas as pl
from jax.experimental.pallas import tpu as pltpu

F32 = jnp.float32
BF16 = jnp.bfloat16

D_MODEL = 2048
N_HEADS = 4
D_HEAD = 256
D_GROUP = N_HEADS * D_HEAD
CHUNK = 128
D_FF = 5632
DEPTH = 1
DEEPNORM_ALPHA = (2.0 * DEPTH) ** 0.25
LN_EPS = 1e-5
NORM_EPS = 1e-6
ROPE_BASE = 10000.0
RET_DECAY_EXP_FWD = 5.0
RET_DECAY_EXP_BWD = 5.5
NEG_INIT = -1e30
HEAD_SCALE = D_HEAD ** -0.5

SUBLANES = 8
LANES = 128
VMEM_LIMIT_BYTES = 56 * 1024 * 1024

ZB_MQ, ZB_MV, ZB_MO, ZB_RQ, ZB_RV, ZB_RG = (N_HEADS * i for i in range(6))
N_ZBLOCKS = 6 * N_HEADS
KB_M, KB_R = 0, N_HEADS
N_KBLOCKS = 2 * N_HEADS

PK_M, PK_E, PK_IW = 0, 2, 4
N_PACK_ROWS = 6


def _sigmoid(x):
    return 1.0 / (1.0 + jnp.exp(-x))


def _log_sigmoid(x):
    return jnp.minimum(x, 0.0) - jnp.log1p(jnp.exp(-jnp.abs(x)))


def _layernorm(v, g, b):
    mu = jnp.mean(v, axis=-1, keepdims=True)
    d = v - mu
    var = jnp.mean(d * d, axis=-1, keepdims=True)
    return d * lax.rsqrt(var + LN_EPS) * g + b


def _params(*semantics):
    return pltpu.CompilerParams(dimension_semantics=semantics, vmem_limit_bytes=VMEM_LIMIT_BYTES)


ADA_TN = 1024


def _ada_kernel(c_ref, w_ref, b_ref, o_ref):
    c = c_ref[...]
    s = (c * _sigmoid(c)).astype(BF16)
    o_ref[...] = jnp.dot(s, w_ref[...].astype(BF16), preferred_element_type=F32) + b_ref[...]


def _ada(c, w_ada, b_ada):
    rows = c.shape[0]
    n_out = w_ada.shape[1]
    return pl.pallas_call(
        _ada_kernel,
        out_shape=jax.ShapeDtypeStruct((rows, n_out), F32),
        grid=(n_out // ADA_TN,),
        in_specs=[
            pl.BlockSpec((rows, D_MODEL), lambda n: (0, 0)),
            pl.BlockSpec((D_MODEL, ADA_TN), lambda n: (0, n)),
            pl.BlockSpec((1, ADA_TN), lambda n: (0, n)),
        ],
        out_specs=pl.BlockSpec((rows, ADA_TN), lambda n: (0, n)),
        compiler_params=_params("arbitrary"),
        name="ada",
    )(c, w_ada, b_ada)


INPROJ_TM = 1024
INPROJ_TN = D_GROUP
CHUNKS_PER_TM = INPROJ_TM // CHUNK
N_ROW_STEPS = N_ZBLOCKS // N_HEADS
N_KEY_STEPS = N_KBLOCKS // N_HEADS

_NT = (((1,), (1,)), ((), ()))


def _inproj_kernel(x_ref, sc_ref, sh_ref, w_ref, wkt_ref, wgi_ref, wgf_ref, cos_ref, sin_ref, cost_ref, sint_ref,
                   z_ref, zk_ref, gi_ref, gf_ref, h_ref):
    n = pl.program_id(1)
    half = D_HEAD // 2

    @pl.when(n == 0)
    def _():
        hb = (x_ref[...] * (1.0 + sc_ref[...]) + sh_ref[...]).astype(BF16)
        h_ref[...] = hb
        gi = lax.dot_general(wgi_ref[...], hb, _NT, preferred_element_type=F32)
        gf = lax.dot_general(wgf_ref[...], hb, _NT, preferred_element_type=F32)
        for c in range(CHUNKS_PER_TM):
            gi_ref[c * SUBLANES:(c + 1) * SUBLANES, :] = gi[:, c * CHUNK:(c + 1) * CHUNK]
            gf_ref[c * SUBLANES:(c + 1) * SUBLANES, :] = gf[:, c * CHUNK:(c + 1) * CHUNK]

    @pl.when(n < N_ROW_STEPS)
    def _():
        res = jnp.dot(h_ref[...], w_ref[...], preferred_element_type=F32)

        def store(fn):
            for c in range(N_HEADS):
                z_ref[c] = fn(res[:, c * D_HEAD:(c + 1) * D_HEAD]).astype(BF16)

        nb = lambda zb: zb // N_HEADS

        @pl.when(n == nb(ZB_MQ))
        def _():
            store(lambda t: t * HEAD_SCALE)

        @pl.when((n == nb(ZB_MV)) | (n == nb(ZB_RV)))
        def _():
            store(lambda t: t)

        @pl.when(n == nb(ZB_MO))
        def _():
            store(_sigmoid)

        @pl.when(n == nb(ZB_RG))
        def _():
            store(lambda t: t * _sigmoid(t))

        @pl.when(n == nb(ZB_RQ))
        def _():
            cos = cos_ref[...]
            sin = sin_ref[...]
            for c in range(N_HEADS):
                a = res[:, c * D_HEAD:c * D_HEAD + half]
                b = res[:, c * D_HEAD + half:(c + 1) * D_HEAD]
                z_ref[c, :, 0:half] = (a * cos - b * sin).astype(BF16)
                z_ref[c, :, half:D_HEAD] = (a * sin + b * cos).astype(BF16)

    @pl.when(n >= N_ROW_STEPS)
    def _():
        res_t = lax.dot_general(wkt_ref[...], h_ref[...], _NT, preferred_element_type=F32)

        @pl.when(n == N_ROW_STEPS + KB_M // N_HEADS)
        def _():
            for c in range(N_HEADS):
                for cc in range(CHUNKS_PER_TM):
                    zk_ref[c, cc] = res_t[c * D_HEAD:(c + 1) * D_HEAD, cc * CHUNK:(cc + 1) * CHUNK].astype(BF16)

        @pl.when(n == N_ROW_STEPS + KB_R // N_HEADS)
        def _():
            cos = cost_ref[...]
            sin = sint_ref[...]
            for c in range(N_HEADS):
                a = res_t[c * D_HEAD:c * D_HEAD + half, :]
                b = res_t[c * D_HEAD + half:(c + 1) * D_HEAD, :]
                ra = ((a * cos - b * sin) * HEAD_SCALE).astype(BF16)
                rb = ((a * sin + b * cos) * HEAD_SCALE).astype(BF16)
                for cc in range(CHUNKS_PER_TM):
                    zk_ref[c, cc, 0:half, :] = ra[:, cc * CHUNK:(cc + 1) * CHUNK]
                    zk_ref[c, cc, half:D_HEAD, :] = rb[:, cc * CHUNK:(cc + 1) * CHUNK]


def _in_proj(x2, sc, sh, w_main, wkt, wgi, wgf, cos, sin, seq):
    tokens = x2.shape[0]
    tm, tn = INPROJ_TM, INPROJ_TN
    tiles_per_seq = seq // tm
    batch_of = lambda i: i // tiles_per_seq
    row_step = lambda n: jnp.minimum(n, N_ROW_STEPS - 1)
    key_step = lambda n: jnp.maximum(n - N_ROW_STEPS, 0)
    half = D_HEAD // 2
    return pl.pallas_call(
        _inproj_kernel,
        out_shape=(
            jax.ShapeDtypeStruct((N_ZBLOCKS, tokens, D_HEAD), BF16),
            jax.ShapeDtypeStruct((N_KBLOCKS, tokens // CHUNK, D_HEAD, CHUNK), BF16),
            jax.ShapeDtypeStruct((tokens // CHUNK * SUBLANES, CHUNK), F32),
            jax.ShapeDtypeStruct((tokens // CHUNK * SUBLANES, CHUNK), F32),
        ),
        grid=(tokens // tm, N_ROW_STEPS + N_KEY_STEPS),
        in_specs=[
            pl.BlockSpec((tm, D_MODEL), lambda i, n: (i, 0)),
            pl.BlockSpec((None, 1, D_MODEL), lambda i, n: (batch_of(i), 0, 0)),
            pl.BlockSpec((None, 1, D_MODEL), lambda i, n: (batch_of(i), 0, 0)),
            pl.BlockSpec((D_MODEL, tn), lambda i, n: (0, row_step(n))),
            pl.BlockSpec((tn, D_MODEL), lambda i, n: (key_step(n), 0)),
            pl.BlockSpec((SUBLANES, D_MODEL), lambda i, n: (0, 0)),
            pl.BlockSpec((SUBLANES, D_MODEL), lambda i, n: (0, 0)),
            pl.BlockSpec((tm, half), lambda i, n: (i % tiles_per_seq, 0)),
            pl.BlockSpec((tm, half), lambda i, n: (i % tiles_per_seq, 0)),
            pl.BlockSpec((half, tm), lambda i, n: (0, i % tiles_per_seq)),
            pl.BlockSpec((half, tm), lambda i, n: (0, i % tiles_per_seq)),
        ],
        out_specs=(
            pl.BlockSpec((N_HEADS, tm, D_HEAD), lambda i, n: (row_step(n), i, 0)),
            pl.BlockSpec((N_HEADS, CHUNKS_PER_TM, D_HEAD, CHUNK), lambda i, n: (key_step(n), i, 0, 0)),
            pl.BlockSpec((CHUNKS_PER_TM * SUBLANES, CHUNK), lambda i, n: (i, 0)),
            pl.BlockSpec((CHUNKS_PER_TM * SUBLANES, CHUNK), lambda i, n: (i, 0)),
        ),
        scratch_shapes=[pltpu.VMEM((tm, D_MODEL), BF16)],
        compiler_params=_params("arbitrary", "arbitrary"),
        name="in_proj",
    )(x2, sc, sh, w_main, wkt, wgi, wgf, cos, sin, cos.T, sin.T)


def _lane_scan(x, op, suffix, lane):
    k = 1
    while k < CHUNK:
        if suffix:
            shifted = pltpu.roll(x, CHUNK - k, axis=1)
            valid = lane < CHUNK - k
        else:
            shifted = pltpu.roll(x, k, axis=1)
            valid = lane >= k
        x = jnp.where(valid, op(x, shifted), x)
        k *= 2
    return x


def _lane_allreduce(x, op):
    k = 1
    while k < CHUNK:
        x = op(x, pltpu.roll(x, k, axis=1))
        k *= 2
    return x


def _gates_kernel(gi_ref, gf_ref, bi_ref, bf_ref, pack_ref, arow_ref, wsrow_ref, decrow_ref,
                  m_s, e_s, iw_s, amax_s, glast_s, mstf_s, mstb_s, *, n_chunks):
    rows = n_chunks * SUBLANES
    lane = lax.broadcasted_iota(jnp.int32, (rows, CHUNK), 1)
    is_bwd = (lax.broadcasted_iota(jnp.int32, (rows, CHUNK), 0) % SUBLANES) >= N_HEADS

    ig = gi_ref[...] + bi_ref[...]
    lf = _log_sigmoid(gf_ref[...] + bf_ref[...])
    g = jnp.where(is_bwd, _lane_scan(lf, jnp.add, True, lane), _lane_scan(lf, jnp.add, False, lane))
    a = ig - g
    cm = jnp.where(is_bwd, _lane_scan(a, jnp.maximum, True, lane), _lane_scan(a, jnp.maximum, False, lane))
    amax = _lane_allreduce(a, jnp.maximum)
    arow_ref[...] = a
    amax_s[...] = amax
    glast_s[...] = _lane_allreduce(lf, jnp.add)

    def rec(i, carry):
        mf, mb = carry
        rf = pl.ds(pl.multiple_of(i * SUBLANES, SUBLANES), SUBLANES)
        rb = pl.ds(pl.multiple_of((n_chunks - 1 - i) * SUBLANES, SUBLANES), SUBLANES)
        mstf_s[rf, :] = mf
        mstb_s[rb, :] = mb
        mf = glast_s[rf, :] + jnp.maximum(mf, amax_s[rf, :])
        mb = glast_s[rb, :] + jnp.maximum(mb, amax_s[rb, :])
        return mf, mb

    init = jnp.full((SUBLANES, CHUNK), NEG_INIT, F32)
    lax.fori_loop(0, n_chunks, rec, (init, init))

    mst = jnp.where(is_bwd, mstb_s[...], mstf_s[...])
    m_row = jnp.maximum(cm, mst)
    m_last = jnp.maximum(amax, mst)
    m_s[...] = m_row
    e_s[...] = jnp.exp(-(g + m_row))
    iw_s[...] = jnp.exp(mst - m_row)
    wsrow_ref[...] = jnp.exp(a - m_last)
    decrow_ref[...] = jnp.exp(mst - m_last)

    top_rows = SUBLANES
    assert N_PACK_ROWS <= top_rows
    top_row = lax.broadcasted_iota(jnp.int32, (top_rows, CHUNK), 0)
    pad = jnp.zeros((CHUNK - top_rows, CHUNK), F32)

    def emit(j, _):
        r0 = pl.multiple_of(j * SUBLANES, SUBLANES)
        tok = pl.ds(pl.multiple_of(j * CHUNK, CHUNK), CHUNK)
        quantities = [s[pl.ds(r0, SUBLANES), :] for s in (m_s, e_s, iw_s)]
        for h in range(N_HEADS):
            top = jnp.zeros((top_rows, CHUNK), F32)
            for qi, qt in enumerate(quantities):
                for direction in range(2):
                    src = N_HEADS * direction + h
                    top = jnp.where(top_row == 2 * qi + direction, qt[src:src + 1, :], top)
            tile = jnp.concatenate([top, pad], axis=0)
            pack_ref[h, tok, :] = tile.T
        return 0

    lax.fori_loop(0, n_chunks, emit, 0)


def _gates(gi, gf, bi_col, bf_col, batch, seq):
    n_chunks = seq // CHUNK
    rows = n_chunks * SUBLANES
    row_layout = jax.ShapeDtypeStruct((batch * rows, CHUNK), F32)
    per_batch = pl.BlockSpec((rows, CHUNK), lambda b: (b, 0))
    shared = pl.BlockSpec((rows, CHUNK), lambda b: (0, 0))
    return pl.pallas_call(
        functools.partial(_gates_kernel, n_chunks=n_chunks),
        out_shape=(jax.ShapeDtypeStruct((batch, N_HEADS, seq, LANES), F32), row_layout, row_layout, row_layout),
        grid=(batch,),
        in_specs=[per_batch, per_batch, shared, shared],
        out_specs=(pl.BlockSpec((None, N_HEADS, seq, LANES), lambda b: (b, 0, 0, 0)),
                   per_batch, per_batch, per_batch),
        scratch_shapes=[pltpu.VMEM((rows, CHUNK), F32) for _ in range(7)],
        compiler_params=_params("arbitrary"),
        name="gates",
    )(gi, gf, bi_col, bf_col)


def _causal_mask(direction):
    row = lax.broadcasted_iota(jnp.int32, (CHUNK, CHUNK), 0)
    col = lax.broadcasted_iota(jnp.int32, (CHUNK, CHUNK), 1)
    return (col <= row) if direction == 0 else (col >= row)


def _chunk_tokens(j):
    return pl.ds(pl.multiple_of(j * CHUNK, CHUNK), CHUNK)


D_STATE = D_HEAD + LANES


def _mlstm_kernel(q_ref, kt_ref, v_ref, o_ref, pack_ref, arow_ref, wsrow_ref, decrow_ref, nw_ref, y_ref,
                  st_s, cst_s, *, n_chunks):
    head = pl.program_id(1)
    ones = jnp.ones((CHUNK, LANES), BF16)

    def v_ext(j):
        return jnp.concatenate([v_ref[_chunk_tokens(j), :], ones], axis=1)

    def gate_row(direction, j):
        return pl.ds(j * SUBLANES + head + N_HEADS * direction, 1)

    st_s[...] = jnp.zeros_like(st_s)

    def state_step(direction, j):
        st = st_s[direction]
        cst_s[direction, j] = st.astype(BF16)
        kw = (kt_ref[j].astype(F32) * wsrow_ref[gate_row(direction, j), :]).astype(BF16)
        dec = decrow_ref[gate_row(direction, j), :]
        dec = jnp.concatenate([dec] * (D_STATE // LANES), axis=1)
        st_s[direction] = dec * st + jnp.dot(kw, v_ext(j), preferred_element_type=F32)

    def pass_a(i, _):
        state_step(0, i)
        state_step(1, n_chunks - 1 - i)
        return 0

    lax.fori_loop(0, n_chunks, pass_a, 0, unroll=4)

    def pass_b(j, _):
        tok = _chunk_tokens(j)
        q = q_ref[tok, :]
        vx = v_ext(j)
        s = jnp.dot(q, kt_ref[j], preferred_element_type=F32)
        pk = pack_ref[tok, :]
        hm = None
        for direction in range(2):
            col = lambda base: pk[:, base + direction:base + direction + 1]
            m_row, e, iw = col(PK_M), col(PK_E), col(PK_IW)
            a = arow_ref[gate_row(direction, j), :]
            sc = (s * jnp.where(_causal_mask(direction), jnp.exp(a - m_row), 0.0)).astype(BF16)
            tot = (jnp.dot(sc, vx, preferred_element_type=F32)
                   + iw * jnp.dot(q, cst_s[direction, j], preferred_element_type=F32))
            r = 1.0 / jnp.maximum(jnp.abs(tot[:, D_HEAD:]), e)
            h = tot[:, :D_HEAD] * jnp.concatenate([r] * (D_HEAD // LANES), axis=1)
            hm = h if hm is None else hm + h
        mu = jnp.mean(hm, axis=-1, keepdims=True)
        d = hm - mu
        var = jnp.mean(d * d, axis=-1, keepdims=True)
        y = d * lax.rsqrt(var + NORM_EPS) * nw_ref[...] * o_ref[tok, :].astype(F32)
        y_ref[tok, :] = y.astype(BF16)
        return 0

    lax.fori_loop(0, n_chunks, pass_b, 0, unroll=4)


def _mixer_specs(seq, n_chunks, row_blocks, key_block):
    zspec = lambda zb: pl.BlockSpec((None, seq, D_HEAD), lambda b, h: (zb + h, b, 0))
    kspec = pl.BlockSpec((None, n_chunks, D_HEAD, CHUNK), lambda b, h: (key_block + h, b, 0, 0))
    q_block, v_block, gate_block = row_blocks
    return [zspec(q_block), kspec, zspec(v_block), zspec(gate_block)]


def _mlstm(z3, zk, pack, arow, wsrow, decrow, norm_w, batch, seq):
    n_chunks = seq // CHUNK
    rows = pl.BlockSpec((n_chunks * SUBLANES, CHUNK), lambda b, h: (b, 0))
    return pl.pallas_call(
        functools.partial(_mlstm_kernel, n_chunks=n_chunks),
        out_shape=jax.ShapeDtypeStruct((N_HEADS, batch * seq, D_HEAD), BF16),
        grid=(batch, N_HEADS),
        in_specs=_mixer_specs(seq, n_chunks, (ZB_MQ, ZB_MV, ZB_MO), KB_M) + [
            pl.BlockSpec((None, None, seq, LANES), lambda b, h: (b, h, 0, 0)),
            rows, rows, rows,
            pl.BlockSpec((None, 1, D_HEAD), lambda b, h: (h, 0, 0)),
        ],
        out_specs=pl.BlockSpec((None, seq, D_HEAD), lambda b, h: (h, b, 0)),
        scratch_shapes=[
            pltpu.VMEM((2, D_HEAD, D_STATE), F32),
            pltpu.VMEM((2, n_chunks, D_HEAD, D_STATE), BF16),
        ],
        compiler_params=_params("arbitrary", "arbitrary"),
        name="mlstm",
    )(z3, zk, z3, z3, pack, arow, wsrow, decrow, norm_w)


def _ret_kernel(q_ref, kt_ref, v_ref, g_ref, dsum_ref, xi_ref, zeta_ref, cd_ref, y_ref, st_s, rst_s, *, n_chunks):
    st_s[...] = jnp.zeros_like(st_s)

    def state_step(direction, j):
        st = st_s[direction]
        rst_s[direction, j] = st.astype(BF16)
        kz = (kt_ref[j].astype(F32) * zeta_ref[direction]).astype(BF16)
        cd = jnp.concatenate([cd_ref[direction]] * (D_HEAD // LANES), axis=1)
        st_s[direction] = cd * st + jnp.dot(kz, v_ref[_chunk_tokens(j), :], preferred_element_type=F32)

    def pass_a(i, _):
        state_step(0, i)
        state_step(1, n_chunks - 1 - i)
        return 0

    lax.fori_loop(0, n_chunks, pass_a, 0, unroll=4)

    def pass_b(j, _):
        tok = _chunk_tokens(j)
        q = q_ref[tok, :]
        sc = (jnp.dot(q, kt_ref[j], preferred_element_type=F32) * dsum_ref[...]).astype(BF16)
        yr = jnp.dot(sc, v_ref[tok, :], preferred_element_type=F32)
        for direction in range(2):
            yr = yr + xi_ref[direction] * jnp.dot(q, rst_s[direction, j], preferred_element_type=F32)
        yr = yr * lax.rsqrt(jnp.mean(yr * yr, axis=-1, keepdims=True) + NORM_EPS)
        y_ref[tok, :] = (yr * g_ref[tok, :].astype(F32)).astype(BF16)
        return 0

    lax.fori_loop(0, n_chunks, pass_b, 0, unroll=8)


def _retention(z3, zk, dsum, xi, zeta, cd, batch, seq):
    n_chunks = seq // CHUNK
    per_head = lambda *tail: pl.BlockSpec((None,) + tail, lambda b, h: (h,) + (0,) * len(tail))
    return pl.pallas_call(
        functools.partial(_ret_kernel, n_chunks=n_chunks),
        out_shape=jax.ShapeDtypeStruct((N_HEADS, batch * seq, D_HEAD), BF16),
        grid=(batch, N_HEADS),
        in_specs=_mixer_specs(seq, n_chunks, (ZB_RQ, ZB_RV, ZB_RG), KB_R) + [
            per_head(CHUNK, CHUNK), per_head(2, CHUNK, D_HEAD), per_head(2, 1, CHUNK), per_head(2, 1, LANES),
        ],
        out_specs=pl.BlockSpec((None, seq, D_HEAD), lambda b, h: (h, b, 0)),
        scratch_shapes=[
            pltpu.VMEM((2, D_HEAD, D_HEAD), F32),
            pltpu.VMEM((2, n_chunks, D_HEAD, D_HEAD), BF16),
        ],
        compiler_params=_params("arbitrary", "arbitrary"),
        name="retention",
    )(z3, zk, z3, z3, dsum, xi, zeta, cd)


OUTPROJ_TM = 512


def _outproj_kernel(ym_ref, yr_ref, x_ref, g_ref, w_ref, lg_ref, lb_ref, o_ref):
    acc = None
    for grp, y_ref in enumerate((ym_ref, yr_ref)):
        for c in range(N_HEADS):
            r0 = grp * D_GROUP + c * D_HEAD
            part = jnp.dot(y_ref[c], w_ref[r0:r0 + D_HEAD, :], preferred_element_type=F32)
            acc = part if acc is None else acc + part
    v = DEEPNORM_ALPHA * x_ref[...] + g_ref[...] * acc
    o_ref[...] = _layernorm(v, lg_ref[...], lb_ref[...])


def _out_proj(ym, yr, x2, g1, w_out, ln_g, ln_b, seq):
    tokens = x2.shape[0]
    tm = OUTPROJ_TM
    tiles_per_seq = seq // tm
    row = pl.BlockSpec((1, D_MODEL), lambda i: (0, 0))
    return pl.pallas_call(
        _outproj_kernel,
        out_shape=jax.ShapeDtypeStruct((tokens, D_MODEL), F32),
        grid=(tokens // tm,),
        in_specs=[
            pl.BlockSpec((N_HEADS, tm, D_HEAD), lambda i: (0, i, 0)),
            pl.BlockSpec((N_HEADS, tm, D_HEAD), lambda i: (0, i, 0)),
            pl.BlockSpec((tm, D_MODEL), lambda i: (i, 0)),
            pl.BlockSpec((None, 1, D_MODEL), lambda i: (i // tiles_per_seq, 0, 0)),
            pl.BlockSpec((D_MODEL, D_MODEL), lambda i: (0, 0)),
            row, row,
        ],
        out_specs=pl.BlockSpec((tm, D_MODEL), lambda i: (i, 0)),
        compiler_params=_params("arbitrary"),
        name="out_proj",
    )(ym, yr, x2, g1, w_out, ln_g, ln_b)


FFN_TM = 512
FFN_TF = 512
FFN_PARTS = 4
HALO = SUBLANES


def _ffn_kernel(x_ref, xp_ref, xn_ref, sc_ref, sh_ref, g_ref, wa_ref, wg_ref, cwa_ref, cwg_ref,
                cba_ref, cbg_ref, wd_ref, lg_ref, lb_ref, o_ref,
                h_s, act0_s, act1_s, acc_s, *, tiles_per_seq, n_f):
    i = pl.program_id(0)
    s = pl.program_id(1)
    tm = FFN_TM
    act_s = (act0_s, act1_s)

    n_parts = FFN_PARTS
    tf_part = FFN_TF // n_parts
    dm_part = D_MODEL // n_parts

    def up_and_down(slot, down_slot):
        h = h_s[...]
        ua = jnp.dot(h, wa_ref[...], preferred_element_type=F32)
        ug = jnp.dot(h, wg_ref[...], preferred_element_type=F32)

        def conv_part(u, cw_ref, cb_ref, cols):
            cw = cw_ref[:, cols]
            return (cw[0:1, :] * u[HALO - 1:HALO - 1 + tm, cols] + cw[1:2, :] * u[HALO:HALO + tm, cols]
                    + cw[2:3, :] * u[HALO + 1:HALO + 1 + tm, cols] + cb_ref[:, cols])

        for part in range(n_parts):
            cols = slice(part * tf_part, (part + 1) * tf_part)
            a = conv_part(ua, cwa_ref, cba_ref, cols)
            g = conv_part(ug, cwg_ref, cbg_ref, cols)
            act_s[slot][:, cols] = (a * _sigmoid(a) * g).astype(BF16)
            if down_slot is not None:
                out_cols = slice(part * dm_part, (part + 1) * dm_part)
                acc_s[:, out_cols] += jnp.dot(act_s[down_slot][...], wd_ref[:, out_cols],
                                              preferred_element_type=F32)

    @pl.when(s == 0)
    def _():
        scale = 1.0 + sc_ref[...]
        shift = sh_ref[...]
        has_prev = (i % tiles_per_seq != 0).astype(F32)
        has_next = (i % tiles_per_seq != tiles_per_seq - 1).astype(F32)
        h_s[0:HALO, :] = ((xp_ref[...] * scale + shift) * has_prev).astype(BF16)
        h_s[HALO:HALO + tm, :] = (x_ref[...] * scale + shift).astype(BF16)
        h_s[HALO + tm:, :] = ((xn_ref[...] * scale + shift) * has_next).astype(BF16)
        acc_s[...] = jnp.zeros_like(acc_s)
        up_and_down(0, None)

    for parity in range(2):
        @pl.when((s >= 1) & (s < n_f) & (s % 2 == parity))
        def _():
            up_and_down(parity, 1 - parity)

    @pl.when(s == n_f)
    def _():
        acc = acc_s[...] + jnp.dot(act_s[(n_f - 1) % 2][...], wd_ref[...], preferred_element_type=F32)
        v = DEEPNORM_ALPHA * x_ref[...] + g_ref[...] * acc
        o_ref[...] = _layernorm(v, lg_ref[...], lb_ref[...])


def _ffn(x1, sc2, sh2, g2, w_up, conv_w, conv_b, w_down, ln_g, ln_b, seq):
    tokens = x1.shape[0]
    tm, tf = FFN_TM, FFN_TF
    tiles_per_seq = seq // tm
    n_f = D_FF // tf
    halo_per_tile = tm // HALO
    n_halo_blocks = tokens // HALO
    blk = lambda s, lag: jnp.clip(s - lag, 0, n_f - 1)
    up_blk = lambda s: blk(s, 0)
    conv_blk = lambda s: blk(s, 0)
    down_blk = lambda s: blk(s, 1)
    mod = pl.BlockSpec((None, 1, D_MODEL), lambda i, s: (i // tiles_per_seq, 0, 0))
    row = pl.BlockSpec((1, D_MODEL), lambda i, s: (0, 0))
    stage_bf16 = pltpu.VMEM((tm, tf), BF16)
    return pl.pallas_call(
        functools.partial(_ffn_kernel, tiles_per_seq=tiles_per_seq, n_f=n_f),
        out_shape=jax.ShapeDtypeStruct((tokens, D_MODEL), F32),
        grid=(tokens // tm, n_f + 1),
        in_specs=[
            pl.BlockSpec((tm, D_MODEL), lambda i, s: (i, 0)),
            pl.BlockSpec((HALO, D_MODEL), lambda i, s: (jnp.maximum(i * halo_per_tile - 1, 0), 0)),
            pl.BlockSpec((HALO, D_MODEL),
                         lambda i, s: (jnp.minimum((i + 1) * halo_per_tile, n_halo_blocks - 1), 0)),
            mod, mod, mod,
            pl.BlockSpec((D_MODEL, tf), lambda i, s: (0, up_blk(s))),
            pl.BlockSpec((D_MODEL, tf), lambda i, s: (0, n_f + up_blk(s))),
            pl.BlockSpec((3, tf), lambda i, s: (0, conv_blk(s))),
            pl.BlockSpec((3, tf), lambda i, s: (0, n_f + conv_blk(s))),
            pl.BlockSpec((1, tf), lambda i, s: (0, conv_blk(s))),
            pl.BlockSpec((1, tf), lambda i, s: (0, n_f + conv_blk(s))),
            pl.BlockSpec((tf, D_MODEL), lambda i, s: (down_blk(s), 0)),
            row, row,
        ],
        out_specs=pl.BlockSpec((tm, D_MODEL), lambda i, s: (i, 0)),
        scratch_shapes=[
            pltpu.VMEM((tm + 2 * HALO, D_MODEL), BF16),
            stage_bf16, stage_bf16,
            pltpu.VMEM((tm, D_MODEL), F32),
        ],
        compiler_params=_params("arbitrary", "arbitrary"),
        name="ffn",
    )(x1, x1, x1, sc2, sh2, g2, w_up, w_up, conv_w, conv_w, conv_b, conv_b, w_down, ln_g, ln_b)


def _rotary_tables(seq):
    half = D_HEAD // 2
    inv = 1.0 / (ROPE_BASE ** jnp.linspace(0.0, 1.0, half, dtype=F32))
    ang = jnp.arange(seq, dtype=F32)[:, None] * inv[None, :]
    return jnp.cos(ang), jnp.sin(ang)


def _retention_tables():
    hd = jnp.arange(N_HEADS, dtype=F32)
    lg = jnp.stack([jnp.log1p(-jnp.exp2(-RET_DECAY_EXP_FWD - hd)),
                    jnp.log1p(-jnp.exp2(-RET_DECAY_EXP_BWD - hd))], axis=1)
    pos = jnp.arange(CHUNK, dtype=F32)
    diff = pos[:, None] - pos[None, :]
    dmat = jnp.where(diff >= 0, jnp.exp(lg[:, :, None, None] * jnp.maximum(diff, 0.0)), 0.0)
    xi = jnp.exp(lg[:, :, None] * (pos + 1.0))
    zeta = jnp.exp(lg[:, :, None] * (CHUNK - 1.0 - pos))
    cd = jnp.exp(lg * CHUNK)
    dsum = dmat[:, 0] + dmat[:, 1, ::-1, ::-1]
    xi = jnp.stack([xi[:, 0], xi[:, 1, ::-1]], axis=1)
    zeta = jnp.stack([zeta[:, 0], zeta[:, 1, ::-1]], axis=1)
    xi = jnp.broadcast_to(xi[..., None], xi.shape + (D_HEAD,))
    zeta = zeta[:, :, None, :]
    cd = jnp.broadcast_to(cd[:, :, None, None], (N_HEADS, 2, 1, LANES))
    return dsum, xi, zeta, cd


def _split_pairs(w):
    w = w.reshape(D_MODEL, N_HEADS, D_HEAD // 2, 2)
    return jnp.concatenate([w[..., 0], w[..., 1]], axis=-1).reshape(D_MODEL, D_GROUP)


def _layout_w_in(w_in):
    sizes = [D_GROUP] * 4 + [2 * N_HEADS] * 2 + [D_GROUP] * 4
    mq, mk, mv, mo, mi, mf, rq, rk, rv, rg = jnp.split(w_in, [int(s) for s in np.cumsum(sizes)[:-1]], axis=1)
    w_main = jnp.concatenate([mq, mv, mo, _split_pairs(rq), rv, rg], axis=1).astype(BF16)
    wkt = jnp.concatenate([mk.T, _split_pairs(rk).T], axis=0).astype(BF16)
    return w_main, wkt, mi.T.astype(BF16), mf.T.astype(BF16)


def _trunk(x, ada, weights, tables):
    batch, seq, _ = x.shape
    (w_main, wkt, wgi, wgf, bi_col, bf_col, norm_w, w_out, ln1_g, ln1_b, w_up, conv_w, conv_b, w_down,
     ln2_g, ln2_b) = weights
    sh1, sc1, g1, sh2, sc2, g2 = (t.reshape(batch, 1, D_MODEL) for t in jnp.split(ada, 6, axis=-1))
    x2 = x.reshape(batch * seq, D_MODEL)
    cos, sin = _rotary_tables(seq)
    n_chunks = seq // CHUNK

    z3, zk, gi, gf = _in_proj(x2, sc1, sh1, w_main, wkt, wgi, wgf, cos, sin, seq)
    pack, arow, wsrow, decrow = _gates(gi, gf, jnp.tile(bi_col, (n_chunks, 1)), jnp.tile(bf_col, (n_chunks, 1)),
                                       batch, seq)
    ym = _mlstm(z3, zk, pack, arow, wsrow, decrow, norm_w, batch, seq)
    yr = _retention(z3, zk, *tables, batch, seq)
    x1 = _out_proj(ym, yr, x2, g1, w_out, ln1_g, ln1_b, seq)
    out = _ffn(x1, sc2, sh2, g2, w_up, conv_w, conv_b, w_down, ln2_g, ln2_b, seq)
    return out.reshape(batch, seq, D_MODEL)


def _prepare_weights(w_in, b_igate, b_fgate, mlstm_norm_w, w_out, ln1_g, ln1_b, w_up, conv_w, conv_b, w_down,
                     ln2_g, ln2_b):
    bcast_col = lambda b: jnp.broadcast_to(b[:, None], (2 * N_HEADS, CHUNK))
    return _layout_w_in(w_in[0]) + (
        bcast_col(b_igate[0]), bcast_col(b_fgate[0]),
        mlstm_norm_w[0].reshape(N_HEADS, 1, D_HEAD), w_out[0].astype(BF16), ln1_g, ln1_b,
        w_up[0].astype(BF16), conv_w[0], conv_b, w_down[0].astype(BF16), ln2_g, ln2_b,
    )


def kernel(x_prompt, x_sample, c_prompt, c_sample, w_ada, b_ada, w_in, b_igate, b_fgate, mlstm_norm_w, w_out,
           ln1_g, ln1_b, w_up, conv_w, conv_b, w_down, ln2_g, ln2_b):
    assert w_ada.shape[0] == DEPTH
    n_prompt, n_sample = c_prompt.shape[0], c_sample.shape[0]
    c_all = jnp.concatenate([c_prompt, c_sample], axis=0)
    pad_rows = -c_all.shape[0] % SUBLANES
    c_all = jnp.pad(c_all, ((0, pad_rows), (0, 0)))
    ada = _ada(c_all, w_ada[0], b_ada[0][None, :])

    weights = _prepare_weights(w_in, b_igate, b_fgate, mlstm_norm_w, w_out, ln1_g, ln1_b, w_up, conv_w, conv_b,
                               w_down, ln2_g, ln2_b)
    tables = _retention_tables()
    y_prompt = _trunk(x_prompt, ada[:n_prompt], weights, tables)
    y_sample = _trunk(x_sample, ada[n_prompt:n_prompt + n_sample], weights, tables)
    return (y_prompt, y_sample)
```

```python
import functools

import jax
import jax.numpy as jnp
import numpy as np
from jax import lax
from jax.experimental import pallas as pl
from jax.experimental.pallas import tpu as pltpu

F32 = jnp.float32
BF16 = jnp.bfloat16

D_MODEL = 2048
N_HEADS = 4
D_HEAD = 256
D_GROUP = N_HEADS * D_HEAD
CHUNK = 128
D_FF = 5632
DEPTH = 1
DEEPNORM_ALPHA = (2.0 * DEPTH) ** 0.25
LN_EPS = 1e-5
NORM_EPS = 1e-6
ROPE_BASE = 10000.0
RET_DECAY_EXP_FWD = 5.0
RET_DECAY_EXP_BWD = 5.5
NEG_INIT = -1e30
HEAD_SCALE = D_HEAD ** -0.5

SUBLANES = 8
LANES = 128
VMEM_LIMIT_BYTES = 56 * 1024 * 1024

ZB_MQ, ZB_MV, ZB_MO, ZB_RQ, ZB_RV, ZB_RG = (N_HEADS * i for i in range(6))
N_ZBLOCKS = 6 * N_HEADS
KB_M, KB_R = 0, N_HEADS
N_KBLOCKS = 2 * N_HEADS

PK_M, PK_E, PK_IW = 0, 2, 4
N_PACK_ROWS = 6


def _sigmoid(x):
    return 1.0 / (1.0 + jnp.exp(-x))


def _log_sigmoid(x):
    return jnp.minimum(x, 0.0) - jnp.log1p(jnp.exp(-jnp.abs(x)))


def _layernorm(v, g, b):
    mu = jnp.mean(v, axis=-1, keepdims=True)
    d = v - mu
    var = jnp.mean(d * d, axis=-1, keepdims=True)
    return d * lax.rsqrt(var + LN_EPS) * g + b


def _params(*semantics):
    return pltpu.CompilerParams(dimension_semantics=semantics, vmem_limit_bytes=VMEM_LIMIT_BYTES)


ADA_TN = 1024


def _ada_kernel(c_ref, w_ref, b_ref, o_ref):
    c = c_ref[...]
    s = (c * _sigmoid(c)).astype(BF16)
    o_ref[...] = jnp.dot(s, w_ref[...].astype(BF16), preferred_element_type=F32) + b_ref[...]


def _ada(c, w_ada, b_ada):
    rows = c.shape[0]
    n_out = w_ada.shape[1]
    return pl.pallas_call(
        _ada_kernel,
        out_shape=jax.ShapeDtypeStruct((rows, n_out), F32),
        grid=(n_out // ADA_TN,),
        in_specs=[
            pl.BlockSpec((rows, D_MODEL), lambda n: (0, 0)),
            pl.BlockSpec((D_MODEL, ADA_TN), lambda n: (0, n)),
            pl.BlockSpec((1, ADA_TN), lambda n: (0, n)),
        ],
        out_specs=pl.BlockSpec((rows, ADA_TN), lambda n: (0, n)),
        compiler_params=_params("arbitrary"),
        name="ada",
    )(c, w_ada, b_ada)


INPROJ_TM = 1024
INPROJ_TN = D_GROUP
CHUNKS_PER_TM = INPROJ_TM // CHUNK
N_ROW_STEPS = N_ZBLOCKS // N_HEADS
N_KEY_STEPS = N_KBLOCKS // N_HEADS

_NT = (((1,), (1,)), ((), ()))


def _inproj_kernel(x_ref, sc_ref, sh_ref, w_ref, wkt_ref, wgi_ref, wgf_ref, cos_ref, sin_ref, cost_ref, sint_ref,
                   z_ref, zk_ref, gi_ref, gf_ref, h_ref):
    n = pl.program_id(1)
    half = D_HEAD // 2

    @pl.when(n == 0)
    def _():
        hb = (x_ref[...] * (1.0 + sc_ref[...]) + sh_ref[...]).astype(BF16)
        h_ref[...] = hb
        gi = lax.dot_general(wgi_ref[...], hb, _NT, preferred_element_type=F32)
        gf = lax.dot_general(wgf_ref[...], hb, _NT, preferred_element_type=F32)
        for c in range(CHUNKS_PER_TM):
            gi_ref[c * SUBLANES:(c + 1) * SUBLANES, :] = gi[:, c * CHUNK:(c + 1) * CHUNK]
            gf_ref[c * SUBLANES:(c + 1) * SUBLANES, :] = gf[:, c * CHUNK:(c + 1) * CHUNK]

    def row_step(epilogue):
        h = h_ref[...]
        for c in range(N_HEADS):
            epilogue(c, jnp.dot(h, w_ref[:, c * D_HEAD:(c + 1) * D_HEAD], preferred_element_type=F32))

    def elementwise(fn):
        def epilogue(c, t):
            z_ref[c] = fn(t).astype(BF16)
        return epilogue

    def rotary(c, t):
        a = t[:, 0:half]
        b = t[:, half:D_HEAD]
        z_ref[c, :, 0:half] = (a * cos_ref[...] - b * sin_ref[...]).astype(BF16)
        z_ref[c, :, half:D_HEAD] = (a * sin_ref[...] + b * cos_ref[...]).astype(BF16)

    nb = lambda zb: zb // N_HEADS
    pl.when(n == nb(ZB_MQ))(lambda: row_step(elementwise(lambda t: t * HEAD_SCALE)))
    pl.when((n == nb(ZB_MV)) | (n == nb(ZB_RV)))(lambda: row_step(elementwise(lambda t: t)))
    pl.when(n == nb(ZB_MO))(lambda: row_step(elementwise(_sigmoid)))
    pl.when(n == nb(ZB_RG))(lambda: row_step(elementwise(lambda t: t * _sigmoid(t))))
    pl.when(n == nb(ZB_RQ))(lambda: row_step(rotary))

    @pl.when(n >= N_ROW_STEPS)
    def _():
        res_t = lax.dot_general(wkt_ref[...], h_ref[...], _NT, preferred_element_type=F32)

        @pl.when(n == N_ROW_STEPS + KB_M // N_HEADS)
        def _():
            for c in range(N_HEADS):
                for cc in range(CHUNKS_PER_TM):
                    zk_ref[c, cc] = res_t[c * D_HEAD:(c + 1) * D_HEAD, cc * CHUNK:(cc + 1) * CHUNK].astype(BF16)

        @pl.when(n == N_ROW_STEPS + KB_R // N_HEADS)
        def _():
            cos = cost_ref[...]
            sin = sint_ref[...]
            for c in range(N_HEADS):
                a = res_t[c * D_HEAD:c * D_HEAD + half, :]
                b = res_t[c * D_HEAD + half:(c + 1) * D_HEAD, :]
                ra = ((a * cos - b * sin) * HEAD_SCALE).astype(BF16)
                rb = ((a * sin + b * cos) * HEAD_SCALE).astype(BF16)
                for cc in range(CHUNKS_PER_TM):
                    zk_ref[c, cc, 0:half, :] = ra[:, cc * CHUNK:(cc + 1) * CHUNK]
                    zk_ref[c, cc, half:D_HEAD, :] = rb[:, cc * CHUNK:(cc + 1) * CHUNK]


def _in_proj(x2, sc, sh, w_main, wkt, wgi, wgf, cos, sin, seq):
    tokens = x2.shape[0]
    tm, tn = INPROJ_TM, INPROJ_TN
    tiles_per_seq = seq // tm
    batch_of = lambda i: i // tiles_per_seq
    row_step = lambda n: jnp.minimum(n, N_ROW_STEPS - 1)
    key_step = lambda n: jnp.maximum(n - N_ROW_STEPS, 0)
    half = D_HEAD // 2
    return pl.pallas_call(
        _inproj_kernel,
        out_shape=(
            jax.ShapeDtypeStruct((N_ZBLOCKS, tokens, D_HEAD), BF16),
            jax.ShapeDtypeStruct((N_KBLOCKS, tokens // CHUNK, D_HEAD, CHUNK), BF16),
            jax.ShapeDtypeStruct((tokens // CHUNK * SUBLANES, CHUNK), F32),
            jax.ShapeDtypeStruct((tokens // CHUNK * SUBLANES, CHUNK), F32),
        ),
        grid=(tokens // tm, N_ROW_STEPS + N_KEY_STEPS),
        in_specs=[
            pl.BlockSpec((tm, D_MODEL), lambda i, n: (i, 0)),
            pl.BlockSpec((None, 1, D_MODEL), lambda i, n: (batch_of(i), 0, 0)),
            pl.BlockSpec((None, 1, D_MODEL), lambda i, n: (batch_of(i), 0, 0)),
            pl.BlockSpec((D_MODEL, tn), lambda i, n: (0, row_step(n))),
            pl.BlockSpec((tn, D_MODEL), lambda i, n: (key_step(n), 0)),
            pl.BlockSpec((SUBLANES, D_MODEL), lambda i, n: (0, 0)),
            pl.BlockSpec((SUBLANES, D_MODEL), lambda i, n: (0, 0)),
            pl.BlockSpec((tm, half), lambda i, n: (i % tiles_per_seq, 0)),
            pl.BlockSpec((tm, half), lambda i, n: (i % tiles_per_seq, 0)),
            pl.BlockSpec((half, tm), lambda i, n: (0, i % tiles_per_seq)),
            pl.BlockSpec((half, tm), lambda i, n: (0, i % tiles_per_seq)),
        ],
        out_specs=(
            pl.BlockSpec((N_HEADS, tm, D_HEAD), lambda i, n: (row_step(n), i, 0)),
            pl.BlockSpec((N_HEADS, CHUNKS_PER_TM, D_HEAD, CHUNK), lambda i, n: (key_step(n), i, 0, 0)),
            pl.BlockSpec((CHUNKS_PER_TM * SUBLANES, CHUNK), lambda i, n: (i, 0)),
            pl.BlockSpec((CHUNKS_PER_TM * SUBLANES, CHUNK), lambda i, n: (i, 0)),
        ),
        scratch_shapes=[pltpu.VMEM((tm, D_MODEL), BF16)],
        compiler_params=_params("arbitrary", "arbitrary"),
        name="in_proj",
    )(x2, sc, sh, w_main, wkt, wgi, wgf, cos, sin, cos.T, sin.T)


def _lane_scan(x, op, suffix, lane):
    k = 1
    while k < CHUNK:
        if suffix:
            shifted = pltpu.roll(x, CHUNK - k, axis=1)
            valid = lane < CHUNK - k
        else:
            shifted = pltpu.roll(x, k, axis=1)
            valid = lane >= k
        x = jnp.where(valid, op(x, shifted), x)
        k *= 2
    return x


def _lane_allreduce(x, op):
    k = 1
    while k < CHUNK:
        x = op(x, pltpu.roll(x, k, axis=1))
        k *= 2
    return x


def _gates_kernel(gi_ref, gf_ref, bi_ref, bf_ref, pack_ref, arow_ref, wsrow_ref, decrow_ref,
                  m_s, e_s, iw_s, amax_s, glast_s, mstf_s, mstb_s, *, n_chunks):
    rows = n_chunks * SUBLANES
    lane = lax.broadcasted_iota(jnp.int32, (rows, CHUNK), 1)
    is_bwd = (lax.broadcasted_iota(jnp.int32, (rows, CHUNK), 0) % SUBLANES) >= N_HEADS

    ig = gi_ref[...] + bi_ref[...]
    lf = _log_sigmoid(gf_ref[...] + bf_ref[...])
    g = jnp.where(is_bwd, _lane_scan(lf, jnp.add, True, lane), _lane_scan(lf, jnp.add, False, lane))
    a = ig - g
    cm = jnp.where(is_bwd, _lane_scan(a, jnp.maximum, True, lane), _lane_scan(a, jnp.maximum, False, lane))
    amax = _lane_allreduce(a, jnp.maximum)
    arow_ref[...] = a
    amax_s[...] = amax
    glast_s[...] = _lane_allreduce(lf, jnp.add)

    def rec(i, carry):
        mf, mb = carry
        rf = pl.ds(pl.multiple_of(i * SUBLANES, SUBLANES), SUBLANES)
        rb = pl.ds(pl.multiple_of((n_chunks - 1 - i) * SUBLANES, SUBLANES), SUBLANES)
        mstf_s[rf, :] = mf
        mstb_s[rb, :] = mb
        mf = glast_s[rf, :] + jnp.maximum(mf, amax_s[rf, :])
        mb = glast_s[rb, :] + jnp.maximum(mb, amax_s[rb, :])
        return mf, mb

    init = jnp.full((SUBLANES, CHUNK), NEG_INIT, F32)
    lax.fori_loop(0, n_chunks, rec, (init, init))

    mst = jnp.where(is_bwd, mstb_s[...], mstf_s[...])
    m_row = jnp.maximum(cm, mst)
    m_last = jnp.maximum(amax, mst)
    m_s[...] = m_row
    e_s[...] = jnp.exp(-(g + m_row))
    iw_s[...] = jnp.exp(mst - m_row)
    wsrow_ref[...] = jnp.exp(a - m_last)
    decrow_ref[...] = jnp.exp(mst - m_last)

    top_rows = SUBLANES
    assert N_PACK_ROWS <= top_rows
    top_row = lax.broadcasted_iota(jnp.int32, (top_rows, CHUNK), 0)
    pad = jnp.zeros((CHUNK - top_rows, CHUNK), F32)

    def emit(j, _):
        r0 = pl.multiple_of(j * SUBLANES, SUBLANES)
        tok = pl.ds(pl.multiple_of(j * CHUNK, CHUNK), CHUNK)
        quantities = [s[pl.ds(r0, SUBLANES), :] for s in (m_s, e_s, iw_s)]
        for h in range(N_HEADS):
            top = jnp.zeros((top_rows, CHUNK), F32)
            for qi, qt in enumerate(quantities):
                for direction in range(2):
                    src = N_HEADS * direction + h
                    top = jnp.where(top_row == 2 * qi + direction, qt[src:src + 1, :], top)
            tile = jnp.concatenate([top, pad], axis=0)
            pack_ref[h, tok, :] = tile.T
        return 0

    lax.fori_loop(0, n_chunks, emit, 0)


def _gates(gi, gf, bi_col, bf_col, batch, seq):
    n_chunks = seq // CHUNK
    rows = n_chunks * SUBLANES
    row_layout = jax.ShapeDtypeStruct((batch * rows, CHUNK), F32)
    per_batch = pl.BlockSpec((rows, CHUNK), lambda b: (b, 0))
    shared = pl.BlockSpec((rows, CHUNK), lambda b: (0, 0))
    return pl.pallas_call(
        functools.partial(_gates_kernel, n_chunks=n_chunks),
        out_shape=(jax.ShapeDtypeStruct((batch, N_HEADS, seq, LANES), F32), row_layout, row_layout, row_layout),
        grid=(batch,),
        in_specs=[per_batch, per_batch, shared, shared],
        out_specs=(pl.BlockSpec((None, N_HEADS, seq, LANES), lambda b: (b, 0, 0, 0)),
                   per_batch, per_batch, per_batch),
        scratch_shapes=[pltpu.VMEM((rows, CHUNK), F32) for _ in range(7)],
        compiler_params=_params("arbitrary"),
        name="gates",
    )(gi, gf, bi_col, bf_col)


def _causal_mask(direction):
    row = lax.broadcasted_iota(jnp.int32, (CHUNK, CHUNK), 0)
    col = lax.broadcasted_iota(jnp.int32, (CHUNK, CHUNK), 1)
    return (col <= row) if direction == 0 else (col >= row)


def _chunk_tokens(j):
    return pl.ds(pl.multiple_of(j * CHUNK, CHUNK), CHUNK)


D_STATE = D_HEAD + LANES


def _mlstm_kernel(q_ref, kt_ref, v_ref, o_ref, pack_ref, arow_ref, wsrow_ref, decrow_ref, nw_ref, y_ref,
                  st_s, cst_s, *, n_chunks):
    head = pl.program_id(1)
    ones = jnp.ones((CHUNK, LANES), BF16)

    def v_ext(j):
        return jnp.concatenate([v_ref[_chunk_tokens(j), :], ones], axis=1)

    def gate_row(direction, j):
        return pl.ds(j * SUBLANES + head + N_HEADS * direction, 1)

    st_s[...] = jnp.zeros_like(st_s)

    def state_step(direction, j):
        st = st_s[direction]
        cst_s[direction, j] = st.astype(BF16)
        kw = (kt_ref[j].astype(F32) * wsrow_ref[gate_row(direction, j), :]).astype(BF16)
        dec = decrow_ref[gate_row(direction, j), :]
        dec = jnp.concatenate([dec] * (D_STATE // LANES), axis=1)
        st_s[direction] = dec * st + jnp.dot(kw, v_ext(j), preferred_element_type=F32)

    def pass_a(i, _):
        state_step(0, i)
        state_step(1, n_chunks - 1 - i)
        return 0

    lax.fori_loop(0, n_chunks, pass_a, 0, unroll=4)

    def pass_b(j, _):
        tok = _chunk_tokens(j)
        q = q_ref[tok, :]
        vx = v_ext(j)
        s = jnp.dot(q, kt_ref[j], preferred_element_type=F32)
        pk = pack_ref[tok, :]
        hm = None
        for direction in range(2):
            col = lambda base: pk[:, base + direction:base + direction + 1]
            m_row, e, iw = col(PK_M), col(PK_E), col(PK_IW)
            a = arow_ref[gate_row(direction, j), :]
            sc = (s * jnp.where(_causal_mask(direction), jnp.exp(a - m_row), 0.0)).astype(BF16)
            tot = (jnp.dot(sc, vx, preferred_element_type=F32)
                   + iw * jnp.dot(q, cst_s[direction, j], preferred_element_type=F32))
            r = 1.0 / jnp.maximum(jnp.abs(tot[:, D_HEAD:]), e)
            h = tot[:, :D_HEAD] * jnp.concatenate([r] * (D_HEAD // LANES), axis=1)
            hm = h if hm is None else hm + h
        mu = jnp.mean(hm, axis=-1, keepdims=True)
        d = hm - mu
        var = jnp.mean(d * d, axis=-1, keepdims=True)
        y = d * lax.rsqrt(var + NORM_EPS) * nw_ref[...] * o_ref[tok, :].astype(F32)
        y_ref[tok, :] = y.astype(BF16)
        return 0

    lax.fori_loop(0, n_chunks, pass_b, 0, unroll=4)


def _mixer_specs(seq, n_chunks, row_blocks, key_block):
    zspec = lambda zb: pl.BlockSpec((None, seq, D_HEAD), lambda b, h: (zb + h, b, 0))
    kspec = pl.BlockSpec((None, n_chunks, D_HEAD, CHUNK), lambda b, h: (key_block + h, b, 0, 0))
    q_block, v_block, gate_block = row_blocks
    return [zspec(q_block), kspec, zspec(v_block), zspec(gate_block)]


def _mlstm(z3, zk, pack, arow, wsrow, decrow, norm_w, batch, seq):
    n_chunks = seq // CHUNK
    rows = pl.BlockSpec((n_chunks * SUBLANES, CHUNK), lambda b, h: (b, 0))
    return pl.pallas_call(
        functools.partial(_mlstm_kernel, n_chunks=n_chunks),
        out_shape=jax.ShapeDtypeStruct((N_HEADS, batch * seq, D_HEAD), BF16),
        grid=(batch, N_HEADS),
        in_specs=_mixer_specs(seq, n_chunks, (ZB_MQ, ZB_MV, ZB_MO), KB_M) + [
            pl.BlockSpec((None, None, seq, LANES), lambda b, h: (b, h, 0, 0)),
            rows, rows, rows,
            pl.BlockSpec((None, 1, D_HEAD), lambda b, h: (h, 0, 0)),
        ],
        out_specs=pl.BlockSpec((None, seq, D_HEAD), lambda b, h: (h, b, 0)),
        scratch_shapes=[
            pltpu.VMEM((2, D_HEAD, D_STATE), F32),
            pltpu.VMEM((2, n_chunks, D_HEAD, D_STATE), BF16),
        ],
        compiler_params=_params("arbitrary", "arbitrary"),
        name="mlstm",
    )(z3, zk, z3, z3, pack, arow, wsrow, decrow, norm_w)


def _ret_kernel(q_ref, kt_ref, v_ref, g_ref, dsum_ref, xi_ref, zeta_ref, cd_ref, y_ref, st_s, rst_s, *, n_chunks):
    st_s[...] = jnp.zeros_like(st_s)

    def state_step(direction, j):
        st = st_s[direction]
        rst_s[direction, j] = st.astype(BF16)
        kz = (kt_ref[j].astype(F32) * zeta_ref[direction]).astype(BF16)
        cd = jnp.concatenate([cd_ref[direction]] * (D_HEAD // LANES), axis=1)
        st_s[direction] = cd * st + jnp.dot(kz, v_ref[_chunk_tokens(j), :], preferred_element_type=F32)

    def pass_a(i, _):
        state_step(0, i)
        state_step(1, n_chunks - 1 - i)
        return 0

    lax.fori_loop(0, n_chunks, pass_a, 0, unroll=4)

    def pass_b(j, _):
        tok = _chunk_tokens(j)
        q = q_ref[tok, :]
        sc = (jnp.dot(q, kt_ref[j], preferred_element_type=F32) * dsum_ref[...]).astype(BF16)
        yr = jnp.dot(sc, v_ref[tok, :], preferred_element_type=F32)
        for direction in range(2):
            yr = yr + xi_ref[direction] * jnp.dot(q, rst_s[direction, j], preferred_element_type=F32)
        yr = yr * lax.rsqrt(jnp.mean(yr * yr, axis=-1, keepdims=True) + NORM_EPS)
        y_ref[tok, :] = (yr * g_ref[tok, :].astype(F32)).astype(BF16)
        return 0

    lax.fori_loop(0, n_chunks, pass_b, 0, unroll=8)


def _retention(z3, zk, dsum, xi, zeta, cd, batch, seq):
    n_chunks = seq // CHUNK
    per_head = lambda *tail: pl.BlockSpec((None,) + tail, lambda b, h: (h,) + (0,) * len(tail))
    return pl.pallas_call(
        functools.partial(_ret_kernel, n_chunks=n_chunks),
        out_shape=jax.ShapeDtypeStruct((N_HEADS, batch * seq, D_HEAD), BF16),
        grid=(batch, N_HEADS),
        in_specs=_mixer_specs(seq, n_chunks, (ZB_RQ, ZB_RV, ZB_RG), KB_R) + [
            per_head(CHUNK, CHUNK), per_head(2, CHUNK, D_HEAD), per_head(2, 1, CHUNK), per_head(2, 1, LANES),
        ],
        out_specs=pl.BlockSpec((None, seq, D_HEAD), lambda b, h: (h, b, 0)),
        scratch_shapes=[
            pltpu.VMEM((2, D_HEAD, D_HEAD), F32),
            pltpu.VMEM((2, n_chunks, D_HEAD, D_HEAD), BF16),
        ],
        compiler_params=_params("arbitrary", "arbitrary"),
        name="retention",
    )(z3, zk, z3, z3, dsum, xi, zeta, cd)


OUTPROJ_TM = 512


def _outproj_kernel(ym_ref, yr_ref, x_ref, g_ref, w_ref, lg_ref, lb_ref, o_ref, y_s):
    for grp, y_ref in enumerate((ym_ref, yr_ref)):
        for c in range(N_HEADS):
            c0 = grp * D_GROUP + c * D_HEAD
            y_s[:, c0:c0 + D_HEAD] = y_ref[c]
    acc = jnp.dot(y_s[...], w_ref[...], preferred_element_type=F32)
    v = DEEPNORM_ALPHA * x_ref[...] + g_ref[...] * acc
    o_ref[...] = _layernorm(v, lg_ref[...], lb_ref[...])


def _out_proj(ym, yr, x2, g1, w_out, ln_g, ln_b, seq):
    tokens = x2.shape[0]
    tm = OUTPROJ_TM
    tiles_per_seq = seq // tm
    row = pl.BlockSpec((1, D_MODEL), lambda i: (0, 0))
    return pl.pallas_call(
        _outproj_kernel,
        out_shape=jax.ShapeDtypeStruct((tokens, D_MODEL), F32),
        grid=(tokens // tm,),
        in_specs=[
            pl.BlockSpec((N_HEADS, tm, D_HEAD), lambda i: (0, i, 0)),
            pl.BlockSpec((N_HEADS, tm, D_HEAD), lambda i: (0, i, 0)),
            pl.BlockSpec((tm, D_MODEL), lambda i: (i, 0)),
            pl.BlockSpec((None, 1, D_MODEL), lambda i: (i // tiles_per_seq, 0, 0)),
            pl.BlockSpec((D_MODEL, D_MODEL), lambda i: (0, 0)),
            row, row,
        ],
        out_specs=pl.BlockSpec((tm, D_MODEL), lambda i: (i, 0)),
        scratch_shapes=[pltpu.VMEM((tm, D_MODEL), BF16)],
        compiler_params=_params("arbitrary"),
        name="out_proj",
    )(ym, yr, x2, g1, w_out, ln_g, ln_b)


FFN_TM = 512
FFN_TF = 512
FFN_PARTS = 1
HALO = SUBLANES


def _ffn_kernel(x_ref, xp_ref, xn_ref, sc_ref, sh_ref, g_ref, wa_ref, wg_ref, cwa_ref, cwg_ref,
                cba_ref, cbg_ref, wd_ref, lg_ref, lb_ref, o_ref, h_s, acc_s, *, tiles_per_seq):
    i = pl.program_id(0)
    f = pl.program_id(1)
    tm = FFN_TM

    @pl.when(f == 0)
    def _():
        scale = 1.0 + sc_ref[...]
        shift = sh_ref[...]
        has_prev = (i % tiles_per_seq != 0).astype(F32)
        has_next = (i % tiles_per_seq != tiles_per_seq - 1).astype(F32)
        h_s[0:HALO, :] = ((xp_ref[...] * scale + shift) * has_prev).astype(BF16)
        h_s[HALO:HALO + tm, :] = (x_ref[...] * scale + shift).astype(BF16)
        h_s[HALO + tm:, :] = ((xn_ref[...] * scale + shift) * has_next).astype(BF16)
        acc_s[...] = jnp.zeros_like(acc_s)

    h = h_s[...]
    part = FFN_TF // FFN_PARTS
    update = None
    for p in range(FFN_PARTS):
        cols = slice(p * part, (p + 1) * part)

        def conv_branch(w_ref, cw_ref, cb_ref):
            u = jnp.dot(h, w_ref[:, cols], preferred_element_type=F32)
            cw = cw_ref[:, cols]
            return (cw[0:1, :] * u[HALO - 1:HALO - 1 + tm, :] + cw[1:2, :] * u[HALO:HALO + tm, :]
                    + cw[2:3, :] * u[HALO + 1:HALO + 1 + tm, :] + cb_ref[:, cols])

        a = conv_branch(wa_ref, cwa_ref, cba_ref)
        g = conv_branch(wg_ref, cwg_ref, cbg_ref)
        act = (a * _sigmoid(a) * g).astype(BF16)
        contrib = jnp.dot(act, wd_ref[cols, :], preferred_element_type=F32)
        update = contrib if update is None else update + contrib
    acc_s[...] += update

    @pl.when(f == pl.num_programs(1) - 1)
    def _():
        v = DEEPNORM_ALPHA * x_ref[...] + g_ref[...] * acc_s[...]
        o_ref[...] = _layernorm(v, lg_ref[...], lb_ref[...])


def _ffn(x1, sc2, sh2, g2, w_up, conv_w, conv_b, w_down, ln_g, ln_b, seq):
    tokens = x1.shape[0]
    tm, tf = FFN_TM, FFN_TF
    tiles_per_seq = seq // tm
    n_f = D_FF // tf
    halo_per_tile = tm // HALO
    n_halo_blocks = tokens // HALO
    mod = pl.BlockSpec((None, 1, D_MODEL), lambda i, f: (i // tiles_per_seq, 0, 0))
    row = pl.BlockSpec((1, D_MODEL), lambda i, f: (0, 0))
    return pl.pallas_call(
        functools.partial(_ffn_kernel, tiles_per_seq=tiles_per_seq),
        out_shape=jax.ShapeDtypeStruct((tokens, D_MODEL), F32),
        grid=(tokens // tm, n_f),
        in_specs=[
            pl.BlockSpec((tm, D_MODEL), lambda i, f: (i, 0)),
            pl.BlockSpec((HALO, D_MODEL), lambda i, f: (jnp.maximum(i * halo_per_tile - 1, 0), 0)),
            pl.BlockSpec((HALO, D_MODEL),
                         lambda i, f: (jnp.minimum((i + 1) * halo_per_tile, n_halo_blocks - 1), 0)),
            mod, mod, mod,
            pl.BlockSpec((D_MODEL, tf), lambda i, f: (0, f)),
            pl.BlockSpec((D_MODEL, tf), lambda i, f: (0, n_f + f)),
            pl.BlockSpec((3, tf), lambda i, f: (0, f)),
            pl.BlockSpec((3, tf), lambda i, f: (0, n_f + f)),
            pl.BlockSpec((1, tf), lambda i, f: (0, f)),
            pl.BlockSpec((1, tf), lambda i, f: (0, n_f + f)),
            pl.BlockSpec((tf, D_MODEL), lambda i, f: (f, 0)),
            row, row,
        ],
        out_specs=pl.BlockSpec((tm, D_MODEL), lambda i, f: (i, 0)),
        scratch_shapes=[
            pltpu.VMEM((tm + 2 * HALO, D_MODEL), BF16),
            pltpu.VMEM((tm, D_MODEL), F32),
        ],
        compiler_params=_params("arbitrary", "arbitrary"),
        name="ffn",
    )(x1, x1, x1, sc2, sh2, g2, w_up, w_up, conv_w, conv_w, conv_b, conv_b, w_down, ln_g, ln_b)


def _rotary_tables(seq):
    half = D_HEAD // 2
    inv = 1.0 / (ROPE_BASE ** jnp.linspace(0.0, 1.0, half, dtype=F32))
    ang = jnp.arange(seq, dtype=F32)[:, None] * inv[None, :]
    return jnp.cos(ang), jnp.sin(ang)


def _retention_tables():
    hd = jnp.arange(N_HEADS, dtype=F32)
    lg = jnp.stack([jnp.log1p(-jnp.exp2(-RET_DECAY_EXP_FWD - hd)),
                    jnp.log1p(-jnp.exp2(-RET_DECAY_EXP_BWD - hd))], axis=1)
    pos = jnp.arange(CHUNK, dtype=F32)
    diff = pos[:, None] - pos[None, :]
    dmat = jnp.where(diff >= 0, jnp.exp(lg[:, :, None, None] * jnp.maximum(diff, 0.0)), 0.0)
    xi = jnp.exp(lg[:, :, None] * (pos + 1.0))
    zeta = jnp.exp(lg[:, :, None] * (CHUNK - 1.0 - pos))
    cd = jnp.exp(lg * CHUNK)
    dsum = dmat[:, 0] + dmat[:, 1, ::-1, ::-1]
    xi = jnp.stack([xi[:, 0], xi[:, 1, ::-1]], axis=1)
    zeta = jnp.stack([zeta[:, 0], zeta[:, 1, ::-1]], axis=1)
    xi = jnp.broadcast_to(xi[..., None], xi.shape + (D_HEAD,))
    zeta = zeta[:, :, None, :]
    cd = jnp.broadcast_to(cd[:, :, None, None], (N_HEADS, 2, 1, LANES))
    return dsum, xi, zeta, cd


def _split_pairs(w):
    w = w.reshape(D_MODEL, N_HEADS, D_HEAD // 2, 2)
    return jnp.concatenate([w[..., 0], w[..., 1]], axis=-1).reshape(D_MODEL, D_GROUP)


def _layout_w_in(w_in):
    sizes = [D_GROUP] * 4 + [2 * N_HEADS] * 2 + [D_GROUP] * 4
    mq, mk, mv, mo, mi, mf, rq, rk, rv, rg = jnp.split(w_in, [int(s) for s in np.cumsum(sizes)[:-1]], axis=1)
    w_main = jnp.concatenate([mq, mv, mo, _split_pairs(rq), rv, rg], axis=1).astype(BF16)
    wkt = jnp.concatenate([mk.T, _split_pairs(rk).T], axis=0).astype(BF16)
    return w_main, wkt, mi.T.astype(BF16), mf.T.astype(BF16)


def _trunk(x, ada, weights, tables):
    batch, seq, _ = x.shape
    (w_main, wkt, wgi, wgf, bi_col, bf_col, norm_w, w_out, ln1_g, ln1_b, w_up, conv_w, conv_b, w_down,
     ln2_g, ln2_b) = weights
    sh1, sc1, g1, sh2, sc2, g2 = (t.reshape(batch, 1, D_MODEL) for t in jnp.split(ada, 6, axis=-1))
    x2 = x.reshape(batch * seq, D_MODEL)
    cos, sin = _rotary_tables(seq)
    n_chunks = seq // CHUNK

    z3, zk, gi, gf = _in_proj(x2, sc1, sh1, w_main, wkt, wgi, wgf, cos, sin, seq)
    pack, arow, wsrow, decrow = _gates(gi, gf, jnp.tile(bi_col, (n_chunks, 1)), jnp.tile(bf_col, (n_chunks, 1)),
                                       batch, seq)
    ym = _mlstm(z3, zk, pack, arow, wsrow, decrow, norm_w, batch, seq)
    yr = _retention(z3, zk, *tables, batch, seq)
    x1 = _out_proj(ym, yr, x2, g1, w_out, ln1_g, ln1_b, seq)
    out = _ffn(x1, sc2, sh2, g2, w_up, conv_w, conv_b, w_down, ln2_g, ln2_b, seq)
    return out.reshape(batch, seq, D_MODEL)


def _prepare_weights(w_in, b_igate, b_fgate, mlstm_norm_w, w_out, ln1_g, ln1_b, w_up, conv_w, conv_b, w_down,
                     ln2_g, ln2_b):
    bcast_col = lambda b: jnp.broadcast_to(b[:, None], (2 * N_HEADS, CHUNK))
    return _layout_w_in(w_in[0]) + (
        bcast_col(b_igate[0]), bcast_col(b_fgate[0]),
        mlstm_norm_w[0].reshape(N_HEADS, 1, D_HEAD), w_out[0].astype(BF16), ln1_g, ln1_b,
        w_up[0].astype(BF16), conv_w[0], conv_b, w_down[0].astype(BF16), ln2_g, ln2_b,
    )


def kernel(x_prompt, x_sample, c_prompt, c_sample, w_ada, b_ada, w_in, b_igate, b_fgate, mlstm_norm_w, w_out,
           ln1_g, ln1_b, w_up, conv_w, conv_b, w_down, ln2_g, ln2_b):
    assert w_ada.shape[0] == DEPTH
    n_prompt, n_sample = c_prompt.shape[0], c_sample.shape[0]
    c_all = jnp.concatenate([c_prompt, c_sample], axis=0)
    pad_rows = -c_all.shape[0] % SUBLANES
    c_all = jnp.pad(c_all, ((0, pad_rows), (0, 0)))
    ada = _ada(c_all, w_ada[0], b_ada[0][None, :])

    weights = _prepare_weights(w_in, b_igate, b_fgate, mlstm_norm_w, w_out, ln1_g, ln1_b, w_up, conv_w, conv_b,
                               w_down, ln2_g, ln2_b)
    tables = _retention_tables()
    y_prompt = _trunk(x_prompt, ada[:n_prompt], weights, tables)
    y_sample = _trunk(x_sample, ada[n_prompt:n_prompt + n_sample], weights, tables)
    return (y_prompt, y_sample)
```

```python
import functools

import jax
import jax.numpy as jnp
import numpy as np
from jax import lax
from jax.experimental import pallas as pl
from jax.experimental.pallas import tpu as pltpu

F32 = jnp.float32
BF16 = jnp.bfloat16

D_MODEL = 2048
N_HEADS = 4
D_HEAD = 256
D_GROUP = N_HEADS * D_HEAD
CHUNK = 128
D_FF = 5632
DEPTH = 1
DEEPNORM_ALPHA = (2.0 * DEPTH) ** 0.25
LN_EPS = 1e-5
NORM_EPS = 1e-6
ROPE_BASE = 10000.0
RET_DECAY_EXP_FWD = 5.0
RET_DECAY_EXP_BWD = 5.5
NEG_INIT = -1e30
HEAD_SCALE = D_HEAD ** -0.5

SUBLANES = 8
LANES = 128
VMEM_LIMIT_BYTES = 56 * 1024 * 1024

ZB_MQ, ZB_MV, ZB_MO, ZB_RQ, ZB_RV, ZB_RG = (N_HEADS * i for i in range(6))
N_ZBLOCKS = 6 * N_HEADS
KB_M, KB_R = 0, N_HEADS
N_KBLOCKS = 2 * N_HEADS

PK_M, PK_E, PK_IW = 0, 2, 4
N_PACK_ROWS = 6


def _sigmoid(x):
    return 1.0 / (1.0 + jnp.exp(-x))


def _log_sigmoid(x):
    return jnp.minimum(x, 0.0) - jnp.log1p(jnp.exp(-jnp.abs(x)))


def _layernorm(v, g, b):
    mu = jnp.mean(v, axis=-1, keepdims=True)
    d = v - mu
    var = jnp.mean(d * d, axis=-1, keepdims=True)
    return d * lax.rsqrt(var + LN_EPS) * g + b


def _params(*semantics):
    return pltpu.CompilerParams(dimension_semantics=semantics, vmem_limit_bytes=VMEM_LIMIT_BYTES)


ADA_TN = 1024


def _ada_kernel(c_ref, w_ref, b_ref, o_ref):
    c = c_ref[...]
    s = (c * _sigmoid(c)).astype(BF16)
    o_ref[...] = jnp.dot(s, w_ref[...].astype(BF16), preferred_element_type=F32) + b_ref[...]


def _ada(c, w_ada, b_ada):
    rows = c.shape[0]
    n_out = w_ada.shape[1]
    return pl.pallas_call(
        _ada_kernel,
        out_shape=jax.ShapeDtypeStruct((rows, n_out), F32),
        grid=(n_out // ADA_TN,),
        in_specs=[
            pl.BlockSpec((rows, D_MODEL), lambda n: (0, 0)),
            pl.BlockSpec((D_MODEL, ADA_TN), lambda n: (0, n)),
            pl.BlockSpec((1, ADA_TN), lambda n: (0, n)),
        ],
        out_specs=pl.BlockSpec((rows, ADA_TN), lambda n: (0, n)),
        compiler_params=_params("arbitrary"),
        name="ada",
    )(c, w_ada, b_ada)


INPROJ_TM = 1024
INPROJ_TN = D_GROUP
CHUNKS_PER_TM = INPROJ_TM // CHUNK
N_ROW_STEPS = N_ZBLOCKS // N_HEADS
N_KEY_STEPS = N_KBLOCKS // N_HEADS


def _inproj_kernel(x_ref, sc_ref, sh_ref, w_ref, wg_ref, cos_ref, sin_ref, z_ref, zk_ref, gi_ref, gf_ref, h_ref):
    n = pl.program_id(1)
    half = D_HEAD // 2
    n_gates = 2 * N_HEADS

    @pl.when(n == 0)
    def _():
        hb = (x_ref[...] * (1.0 + sc_ref[...]) + sh_ref[...]).astype(BF16)
        h_ref[...] = hb
        g = jnp.dot(hb, wg_ref[...], preferred_element_type=F32)
        for c in range(CHUNKS_PER_TM):
            gt = g[c * CHUNK:(c + 1) * CHUNK, :].T
            gi_ref[c * SUBLANES:(c + 1) * SUBLANES, :] = gt[0:n_gates, :]
            gf_ref[c * SUBLANES:(c + 1) * SUBLANES, :] = gt[n_gates:2 * n_gates, :]

    def step(epilogue):
        h = h_ref[...]
        for c in range(N_HEADS):
            epilogue(c, jnp.dot(h, w_ref[:, c * D_HEAD:(c + 1) * D_HEAD], preferred_element_type=F32))

    def elementwise(fn):
        def epilogue(c, t):
            z_ref[c] = fn(t).astype(BF16)
        return epilogue

    def rotate(t):
        a = t[:, 0:half]
        b = t[:, half:D_HEAD]
        return a * cos_ref[...] - b * sin_ref[...], a * sin_ref[...] + b * cos_ref[...]

    def rotary(c, t):
        ra, rb = rotate(t)
        z_ref[c, :, 0:half] = ra.astype(BF16)
        z_ref[c, :, half:D_HEAD] = rb.astype(BF16)

    def keys(halves_fn):
        def epilogue(c, t):
            for lo, part in zip((0, half), halves_fn(t)):
                for cc in range(CHUNKS_PER_TM):
                    zk_ref[c, cc, lo:lo + half, :] = part[cc * CHUNK:(cc + 1) * CHUNK, :].T.astype(BF16)
        return epilogue

    nb = lambda zb: zb // N_HEADS
    kb = lambda k: N_ROW_STEPS + k // N_HEADS
    pl.when(n == nb(ZB_MQ))(lambda: step(elementwise(lambda t: t * HEAD_SCALE)))
    pl.when((n == nb(ZB_MV)) | (n == nb(ZB_RV)))(lambda: step(elementwise(lambda t: t)))
    pl.when(n == nb(ZB_MO))(lambda: step(elementwise(_sigmoid)))
    pl.when(n == nb(ZB_RG))(lambda: step(elementwise(lambda t: t * _sigmoid(t))))
    pl.when(n == nb(ZB_RQ))(lambda: step(rotary))
    pl.when(n == kb(KB_M))(lambda: step(keys(lambda t: (t[:, 0:half], t[:, half:D_HEAD]))))
    pl.when(n == kb(KB_R))(lambda: step(keys(lambda t: tuple(r * HEAD_SCALE for r in rotate(t)))))


def _in_proj(x2, sc, sh, w_main, wg, cos, sin, seq):
    tokens = x2.shape[0]
    tm, tn = INPROJ_TM, INPROJ_TN
    tiles_per_seq = seq // tm
    batch_of = lambda i: i // tiles_per_seq
    row_step = lambda n: jnp.minimum(n, N_ROW_STEPS - 1)
    key_step = lambda n: jnp.maximum(n - N_ROW_STEPS, 0)
    half = D_HEAD // 2
    return pl.pallas_call(
        _inproj_kernel,
        out_shape=(
            jax.ShapeDtypeStruct((N_ZBLOCKS, tokens, D_HEAD), BF16),
            jax.ShapeDtypeStruct((N_KBLOCKS, tokens // CHUNK, D_HEAD, CHUNK), BF16),
            jax.ShapeDtypeStruct((tokens // CHUNK * SUBLANES, CHUNK), F32),
            jax.ShapeDtypeStruct((tokens // CHUNK * SUBLANES, CHUNK), F32),
        ),
        grid=(tokens // tm, N_ROW_STEPS + N_KEY_STEPS),
        in_specs=[
            pl.BlockSpec((tm, D_MODEL), lambda i, n: (i, 0)),
            pl.BlockSpec((None, 1, D_MODEL), lambda i, n: (batch_of(i), 0, 0)),
            pl.BlockSpec((None, 1, D_MODEL), lambda i, n: (batch_of(i), 0, 0)),
            pl.BlockSpec((D_MODEL, tn), lambda i, n: (0, n)),
            pl.BlockSpec((D_MODEL, LANES), lambda i, n: (0, 0)),
            pl.BlockSpec((tm, half), lambda i, n: (i % tiles_per_seq, 0)),
            pl.BlockSpec((tm, half), lambda i, n: (i % tiles_per_seq, 0)),
        ],
        out_specs=(
            pl.BlockSpec((N_HEADS, tm, D_HEAD), lambda i, n: (row_step(n), i, 0)),
            pl.BlockSpec((N_HEADS, CHUNKS_PER_TM, D_HEAD, CHUNK), lambda i, n: (key_step(n), i, 0, 0)),
            pl.BlockSpec((CHUNKS_PER_TM * SUBLANES, CHUNK), lambda i, n: (i, 0)),
            pl.BlockSpec((CHUNKS_PER_TM * SUBLANES, CHUNK), lambda i, n: (i, 0)),
        ),
        scratch_shapes=[pltpu.VMEM((tm, D_MODEL), BF16)],
        compiler_params=_params("arbitrary", "arbitrary"),
        name="in_proj",
    )(x2, sc, sh, w_main, wg, cos, sin)


def _lane_scan(x, op, suffix, lane):
    k = 1
    while k < CHUNK:
        if suffix:
            shifted = pltpu.roll(x, CHUNK - k, axis=1)
            valid = lane < CHUNK - k
        else:
            shifted = pltpu.roll(x, k, axis=1)
            valid = lane >= k
        x = jnp.where(valid, op(x, shifted), x)
        k *= 2
    return x


def _lane_allreduce(x, op):
    k = 1
    while k < CHUNK:
        x = op(x, pltpu.roll(x, k, axis=1))
        k *= 2
    return x


def _gates_kernel(gi_ref, gf_ref, bi_ref, bf_ref, pack_ref, arow_ref, wsrow_ref, decrow_ref,
                  m_s, e_s, iw_s, amax_s, glast_s, mstf_s, mstb_s, *, n_chunks):
    rows = n_chunks * SUBLANES
    lane = lax.broadcasted_iota(jnp.int32, (rows, CHUNK), 1)
    is_bwd = (lax.broadcasted_iota(jnp.int32, (rows, CHUNK), 0) % SUBLANES) >= N_HEADS

    ig = gi_ref[...] + bi_ref[...]
    lf = _log_sigmoid(gf_ref[...] + bf_ref[...])
    g = jnp.where(is_bwd, _lane_scan(lf, jnp.add, True, lane), _lane_scan(lf, jnp.add, False, lane))
    a = ig - g
    cm = jnp.where(is_bwd, _lane_scan(a, jnp.maximum, True, lane), _lane_scan(a, jnp.maximum, False, lane))
    amax = _lane_allreduce(a, jnp.maximum)
    arow_ref[...] = a
    amax_s[...] = amax
    glast_s[...] = _lane_allreduce(lf, jnp.add)

    def rec(i, carry):
        mf, mb = carry
        rf = pl.ds(pl.multiple_of(i * SUBLANES, SUBLANES), SUBLANES)
        rb = pl.ds(pl.multiple_of((n_chunks - 1 - i) * SUBLANES, SUBLANES), SUBLANES)
        mstf_s[rf, :] = mf
        mstb_s[rb, :] = mb
        mf = glast_s[rf, :] + jnp.maximum(mf, amax_s[rf, :])
        mb = glast_s[rb, :] + jnp.maximum(mb, amax_s[rb, :])
        return mf, mb

    init = jnp.full((SUBLANES, CHUNK), NEG_INIT, F32)
    lax.fori_loop(0, n_chunks, rec, (init, init))

    mst = jnp.where(is_bwd, mstb_s[...], mstf_s[...])
    m_row = jnp.maximum(cm, mst)
    m_last = jnp.maximum(amax, mst)
    m_s[...] = m_row
    e_s[...] = jnp.exp(-(g + m_row))
    iw_s[...] = jnp.exp(mst - m_row)
    wsrow_ref[...] = jnp.exp(a - m_last)
    decrow_ref[...] = jnp.exp(mst - m_last)

    top_rows = SUBLANES
    assert N_PACK_ROWS <= top_rows
    top_row = lax.broadcasted_iota(jnp.int32, (top_rows, CHUNK), 0)
    pad = jnp.zeros((CHUNK - top_rows, CHUNK), F32)

    def emit(j, _):
        r0 = pl.multiple_of(j * SUBLANES, SUBLANES)
        tok = pl.ds(pl.multiple_of(j * CHUNK, CHUNK), CHUNK)
        quantities = [s[pl.ds(r0, SUBLANES), :] for s in (m_s, e_s, iw_s)]
        for h in range(N_HEADS):
            top = jnp.zeros((top_rows, CHUNK), F32)
            for qi, qt in enumerate(quantities):
                for direction in range(2):
                    src = N_HEADS * direction + h
                    top = jnp.where(top_row == 2 * qi + direction, qt[src:src + 1, :], top)
            tile = jnp.concatenate([top, pad], axis=0)
            pack_ref[h, tok, :] = tile.T
        return 0

    lax.fori_loop(0, n_chunks, emit, 0)


def _gates(gi, gf, bi_col, bf_col, batch, seq):
    n_chunks = seq // CHUNK
    rows = n_chunks * SUBLANES
    row_layout = jax.ShapeDtypeStruct((batch * rows, CHUNK), F32)
    per_batch = pl.BlockSpec((rows, CHUNK), lambda b: (b, 0))
    shared = pl.BlockSpec((rows, CHUNK), lambda b: (0, 0))
    return pl.pallas_call(
        functools.partial(_gates_kernel, n_chunks=n_chunks),
        out_shape=(jax.ShapeDtypeStruct((batch, N_HEADS, seq, LANES), F32), row_layout, row_layout, row_layout),
        grid=(batch,),
        in_specs=[per_batch, per_batch, shared, shared],
        out_specs=(pl.BlockSpec((None, N_HEADS, seq, LANES), lambda b: (b, 0, 0, 0)),
                   per_batch, per_batch, per_batch),
        scratch_shapes=[pltpu.VMEM((rows, CHUNK), F32) for _ in range(7)],
        compiler_params=_params("arbitrary"),
        name="gates",
    )(gi, gf, bi_col, bf_col)


def _causal_mask(direction):
    row = lax.broadcasted_iota(jnp.int32, (CHUNK, CHUNK), 0)
    col = lax.broadcasted_iota(jnp.int32, (CHUNK, CHUNK), 1)
    return (col <= row) if direction == 0 else (col >= row)


def _chunk_tokens(j):
    return pl.ds(pl.multiple_of(j * CHUNK, CHUNK), CHUNK)


D_STATE = D_HEAD + LANES


def _mlstm_kernel(q_ref, kt_ref, v_ref, o_ref, pack_ref, arow_ref, wsrow_ref, decrow_ref, nw_ref, y_ref,
                  st_s, cst_s, *, n_chunks):
    head = pl.program_id(1)
    ones = jnp.ones((CHUNK, LANES), BF16)

    def v_ext(j):
        return jnp.concatenate([v_ref[_chunk_tokens(j), :], ones], axis=1)

    def gate_row(direction, j):
        return pl.ds(j * SUBLANES + head + N_HEADS * direction, 1)

    st_s[...] = jnp.zeros_like(st_s)

    def state_step(direction, j):
        st = st_s[direction]
        cst_s[direction, j] = st.astype(BF16)
        kw = (kt_ref[j].astype(F32) * wsrow_ref[gate_row(direction, j), :]).astype(BF16)
        dec = decrow_ref[gate_row(direction, j), :]
        dec = jnp.concatenate([dec] * (D_STATE // LANES), axis=1)
        st_s[direction] = dec * st + jnp.dot(kw, v_ext(j), preferred_element_type=F32)

    def pass_a(i, _):
        state_step(0, i)
        state_step(1, n_chunks - 1 - i)
        return 0

    lax.fori_loop(0, n_chunks, pass_a, 0, unroll=4)

    def pass_b(j, _):
        tok = _chunk_tokens(j)
        q = q_ref[tok, :]
        vx = v_ext(j)
        s = jnp.dot(q, kt_ref[j], preferred_element_type=F32)
        pk = pack_ref[tok, :]
        hm = None
        for direction in range(2):
            col = lambda base: pk[:, base + direction:base + direction + 1]
            m_row, e, iw = col(PK_M), col(PK_E), col(PK_IW)
            a = arow_ref[gate_row(direction, j), :]
            sc = (s * jnp.where(_causal_mask(direction), jnp.exp(a - m_row), 0.0)).astype(BF16)
            tot = (jnp.dot(sc, vx, preferred_element_type=F32)
                   + iw * jnp.dot(q, cst_s[direction, j], preferred_element_type=F32))
            r = 1.0 / jnp.maximum(jnp.abs(tot[:, D_HEAD:]), e)
            h = tot[:, :D_HEAD] * jnp.concatenate([r] * (D_HEAD // LANES), axis=1)
            hm = h if hm is None else hm + h
        mu = jnp.mean(hm, axis=-1, keepdims=True)
        d = hm - mu
        var = jnp.mean(d * d, axis=-1, keepdims=True)
        y = d * lax.rsqrt(var + NORM_EPS) * nw_ref[...] * o_ref[tok, :].astype(F32)
        y_ref[tok, :] = y.astype(BF16)
        return 0

    lax.fori_loop(0, n_chunks, pass_b, 0, unroll=4)


def _mixer_specs(seq, n_chunks, row_blocks, key_block):
    zspec = lambda zb: pl.BlockSpec((None, seq, D_HEAD), lambda b, h: (zb + h, b, 0))
    kspec = pl.BlockSpec((None, n_chunks, D_HEAD, CHUNK), lambda b, h: (key_block + h, b, 0, 0))
    q_block, v_block, gate_block = row_blocks
    return [zspec(q_block), kspec, zspec(v_block), zspec(gate_block)]


def _mlstm(z3, zk, pack, arow, wsrow, decrow, norm_w, batch, seq):
    n_chunks = seq // CHUNK
    rows = pl.BlockSpec((n_chunks * SUBLANES, CHUNK), lambda b, h: (b, 0))
    return pl.pallas_call(
        functools.partial(_mlstm_kernel, n_chunks=n_chunks),
        out_shape=jax.ShapeDtypeStruct((N_HEADS, batch * seq, D_HEAD), BF16),
        grid=(batch, N_HEADS),
        in_specs=_mixer_specs(seq, n_chunks, (ZB_MQ, ZB_MV, ZB_MO), KB_M) + [
            pl.BlockSpec((None, None, seq, LANES), lambda b, h: (b, h, 0, 0)),
            rows, rows, rows,
            pl.BlockSpec((None, 1, D_HEAD), lambda b, h: (h, 0, 0)),
        ],
        out_specs=pl.BlockSpec((None, seq, D_HEAD), lambda b, h: (h, b, 0)),
        scratch_shapes=[
            pltpu.VMEM((2, D_HEAD, D_STATE), F32),
            pltpu.VMEM((2, n_chunks, D_HEAD, D_STATE), BF16),
        ],
        compiler_params=_params("arbitrary", "arbitrary"),
        name="mlstm",
    )(z3, zk, z3, z3, pack, arow, wsrow, decrow, norm_w)


def _ret_kernel(q_ref, kt_ref, v_ref, g_ref, dsum_ref, xi_ref, zeta_ref, cd_ref, y_ref, st_s, rst_s, *, n_chunks):
    st_s[...] = jnp.zeros_like(st_s)

    def state_step(direction, j):
        st = st_s[direction]
        rst_s[direction, j] = st.astype(BF16)
        kz = (kt_ref[j].astype(F32) * zeta_ref[direction]).astype(BF16)
        cd = jnp.concatenate([cd_ref[direction]] * (D_HEAD // LANES), axis=1)
        st_s[direction] = cd * st + jnp.dot(kz, v_ref[_chunk_tokens(j), :], preferred_element_type=F32)

    def pass_a(i, _):
        state_step(0, i)
        state_step(1, n_chunks - 1 - i)
        return 0

    lax.fori_loop(0, n_chunks, pass_a, 0, unroll=4)

    def pass_b(j, _):
        tok = _chunk_tokens(j)
        q = q_ref[tok, :]
        sc = (jnp.dot(q, kt_ref[j], preferred_element_type=F32) * dsum_ref[...]).astype(BF16)
        yr = jnp.dot(sc, v_ref[tok, :], preferred_element_type=F32)
        for direction in range(2):
            yr = yr + xi_ref[direction] * jnp.dot(q, rst_s[direction, j], preferred_element_type=F32)
        yr = yr * lax.rsqrt(jnp.mean(yr * yr, axis=-1, keepdims=True) + NORM_EPS)
        y_ref[tok, :] = (yr * g_ref[tok, :].astype(F32)).astype(BF16)
        return 0

    lax.fori_loop(0, n_chunks, pass_b, 0, unroll=8)


def _retention(z3, zk, dsum, xi, zeta, cd, batch, seq):
    n_chunks = seq // CHUNK
    per_head = lambda *tail: pl.BlockSpec((None,) + tail, lambda b, h: (h,) + (0,) * len(tail))
    return pl.pallas_call(
        functools.partial(_ret_kernel, n_chunks=n_chunks),
        out_shape=jax.ShapeDtypeStruct((N_HEADS, batch * seq, D_HEAD), BF16),
        grid=(batch, N_HEADS),
        in_specs=_mixer_specs(seq, n_chunks, (ZB_RQ, ZB_RV, ZB_RG), KB_R) + [
            per_head(CHUNK, CHUNK), per_head(2, CHUNK, D_HEAD), per_head(2, 1, CHUNK), per_head(2, 1, LANES),
        ],
        out_specs=pl.BlockSpec((None, seq, D_HEAD), lambda b, h: (h, b, 0)),
        scratch_shapes=[
            pltpu.VMEM((2, D_HEAD, D_HEAD), F32),
            pltpu.VMEM((2, n_chunks, D_HEAD, D_HEAD), BF16),
        ],
        compiler_params=_params("arbitrary", "arbitrary"),
        name="retention",
    )(z3, zk, z3, z3, dsum, xi, zeta, cd)


OUTPROJ_TM = 512


def _outproj_kernel(ym_ref, yr_ref, x_ref, g_ref, w_ref, lg_ref, lb_ref, o_ref, y_s):
    for grp, y_ref in enumerate((ym_ref, yr_ref)):
        for c in range(N_HEADS):
            c0 = grp * D_GROUP + c * D_HEAD
            y_s[:, c0:c0 + D_HEAD] = y_ref[c]
    acc = jnp.dot(y_s[...], w_ref[...], preferred_element_type=F32)
    v = DEEPNORM_ALPHA * x_ref[...] + g_ref[...] * acc
    o_ref[...] = _layernorm(v, lg_ref[...], lb_ref[...])


def _out_proj(ym, yr, x2, g1, w_out, ln_g, ln_b, seq):
    tokens = x2.shape[0]
    tm = OUTPROJ_TM
    tiles_per_seq = seq // tm
    row = pl.BlockSpec((1, D_MODEL), lambda i: (0, 0))
    return pl.pallas_call(
        _outproj_kernel,
        out_shape=jax.ShapeDtypeStruct((tokens, D_MODEL), F32),
        grid=(tokens // tm,),
        in_specs=[
            pl.BlockSpec((N_HEADS, tm, D_HEAD), lambda i: (0, i, 0)),
            pl.BlockSpec((N_HEADS, tm, D_HEAD), lambda i: (0, i, 0)),
            pl.BlockSpec((tm, D_MODEL), lambda i: (i, 0)),
            pl.BlockSpec((None, 1, D_MODEL), lambda i: (i // tiles_per_seq, 0, 0)),
            pl.BlockSpec((D_MODEL, D_MODEL), lambda i: (0, 0)),
            row, row,
        ],
        out_specs=pl.BlockSpec((tm, D_MODEL), lambda i: (i, 0)),
        scratch_shapes=[pltpu.VMEM((tm, D_MODEL), BF16)],
        compiler_params=_params("arbitrary"),
        name="out_proj",
    )(ym, yr, x2, g1, w_out, ln_g, ln_b)


FFN_TM = 512
FFN_TF = 512
FFN_PARTS = 2
HALO = SUBLANES


def _ffn_kernel(x_ref, xp_ref, xn_ref, sc_ref, sh_ref, g_ref, wa_ref, wg_ref, cwa_ref, cwg_ref,
                cba_ref, cbg_ref, wd_ref, lg_ref, lb_ref, o_ref, h_s, acc_s, *, tiles_per_seq):
    i = pl.program_id(0)
    f = pl.program_id(1)
    tm = FFN_TM

    @pl.when(f == 0)
    def _():
        scale = 1.0 + sc_ref[...]
        shift = sh_ref[...]
        has_prev = (i % tiles_per_seq != 0).astype(F32)
        has_next = (i % tiles_per_seq != tiles_per_seq - 1).astype(F32)
        h_s[0:HALO, :] = ((xp_ref[...] * scale + shift) * has_prev).astype(BF16)
        h_s[HALO:HALO + tm, :] = (x_ref[...] * scale + shift).astype(BF16)
        h_s[HALO + tm:, :] = ((xn_ref[...] * scale + shift) * has_next).astype(BF16)
        acc_s[...] = jnp.zeros_like(acc_s)

    rows = tm // FFN_PARTS
    ups = []
    for p in range(FFN_PARTS):
        h = h_s[p * rows:p * rows + rows + 2 * HALO, :]
        ups.append((jnp.dot(h, wa_ref[...], preferred_element_type=F32),
                    jnp.dot(h, wg_ref[...], preferred_element_type=F32)))

    def conv(u, cw_ref, cb_ref):
        cw = cw_ref[...]
        return (cw[0:1, :] * u[HALO - 1:HALO - 1 + rows, :] + cw[1:2, :] * u[HALO:HALO + rows, :]
                + cw[2:3, :] * u[HALO + 1:HALO + 1 + rows, :] + cb_ref[...])

    for p, (ua, ug) in enumerate(ups):
        a = conv(ua, cwa_ref, cba_ref)
        g = conv(ug, cwg_ref, cbg_ref)
        act = (a * _sigmoid(a) * g).astype(BF16)
        acc_s[p * rows:(p + 1) * rows, :] += jnp.dot(act, wd_ref[...], preferred_element_type=F32)

    @pl.when(f == pl.num_programs(1) - 1)
    def _():
        v = DEEPNORM_ALPHA * x_ref[...] + g_ref[...] * acc_s[...]
        o_ref[...] = _layernorm(v, lg_ref[...], lb_ref[...])


def _ffn(x1, sc2, sh2, g2, w_up, conv_w, conv_b, w_down, ln_g, ln_b, seq):
    tokens = x1.shape[0]
    tm, tf = FFN_TM, FFN_TF
    tiles_per_seq = seq // tm
    n_f = D_FF // tf
    halo_per_tile = tm // HALO
    n_halo_blocks = tokens // HALO
    mod = pl.BlockSpec((None, 1, D_MODEL), lambda i, f: (i // tiles_per_seq, 0, 0))
    row = pl.BlockSpec((1, D_MODEL), lambda i, f: (0, 0))
    return pl.pallas_call(
        functools.partial(_ffn_kernel, tiles_per_seq=tiles_per_seq),
        out_shape=jax.ShapeDtypeStruct((tokens, D_MODEL), F32),
        grid=(tokens // tm, n_f),
        in_specs=[
            pl.BlockSpec((tm, D_MODEL), lambda i, f: (i, 0)),
            pl.BlockSpec((HALO, D_MODEL), lambda i, f: (jnp.maximum(i * halo_per_tile - 1, 0), 0)),
            pl.BlockSpec((HALO, D_MODEL),
                         lambda i, f: (jnp.minimum((i + 1) * halo_per_tile, n_halo_blocks - 1), 0)),
            mod, mod, mod,
            pl.BlockSpec((D_MODEL, tf), lambda i, f: (0, f)),
            pl.BlockSpec((D_MODEL, tf), lambda i, f: (0, n_f + f)),
            pl.BlockSpec((3, tf), lambda i, f: (0, f)),
            pl.BlockSpec((3, tf), lambda i, f: (0, n_f + f)),
            pl.BlockSpec((1, tf), lambda i, f: (0, f)),
            pl.BlockSpec((1, tf), lambda i, f: (0, n_f + f)),
            pl.BlockSpec((tf, D_MODEL), lambda i, f: (f, 0)),
            row, row,
        ],
        out_specs=pl.BlockSpec((tm, D_MODEL), lambda i, f: (i, 0)),
        scratch_shapes=[
            pltpu.VMEM((tm + 2 * HALO, D_MODEL), BF16),
            pltpu.VMEM((tm, D_MODEL), F32),
        ],
        compiler_params=_params("arbitrary", "arbitrary"),
        name="ffn",
    )(x1, x1, x1, sc2, sh2, g2, w_up, w_up, conv_w, conv_w, conv_b, conv_b, w_down, ln_g, ln_b)


def _rotary_tables(seq):
    half = D_HEAD // 2
    inv = 1.0 / (ROPE_BASE ** jnp.linspace(0.0, 1.0, half, dtype=F32))
    ang = jnp.arange(seq, dtype=F32)[:, None] * inv[None, :]
    return jnp.cos(ang), jnp.sin(ang)


def _retention_tables():
    hd = jnp.arange(N_HEADS, dtype=F32)
    lg = jnp.stack([jnp.log1p(-jnp.exp2(-RET_DECAY_EXP_FWD - hd)),
                    jnp.log1p(-jnp.exp2(-RET_DECAY_EXP_BWD - hd))], axis=1)
    pos = jnp.arange(CHUNK, dtype=F32)
    diff = pos[:, None] - pos[None, :]
    dmat = jnp.where(diff >= 0, jnp.exp(lg[:, :, None, None] * jnp.maximum(diff, 0.0)), 0.0)
    xi = jnp.exp(lg[:, :, None] * (pos + 1.0))
    zeta = jnp.exp(lg[:, :, None] * (CHUNK - 1.0 - pos))
    cd = jnp.exp(lg * CHUNK)
    dsum = dmat[:, 0] + dmat[:, 1, ::-1, ::-1]
    xi = jnp.stack([xi[:, 0], xi[:, 1, ::-1]], axis=1)
    zeta = jnp.stack([zeta[:, 0], zeta[:, 1, ::-1]], axis=1)
    xi = jnp.broadcast_to(xi[..., None], xi.shape + (D_HEAD,))
    zeta = zeta[:, :, None, :]
    cd = jnp.broadcast_to(cd[:, :, None, None], (N_HEADS, 2, 1, LANES))
    return dsum, xi, zeta, cd


def _split_pairs(w):
    w = w.reshape(D_MODEL, N_HEADS, D_HEAD // 2, 2)
    return jnp.concatenate([w[..., 0], w[..., 1]], axis=-1).reshape(D_MODEL, D_GROUP)


def _layout_w_in(w_in):
    sizes = [D_GROUP] * 4 + [2 * N_HEADS] * 2 + [D_GROUP] * 4
    mq, mk, mv, mo, mi, mf, rq, rk, rv, rg = jnp.split(w_in, [int(s) for s in np.cumsum(sizes)[:-1]], axis=1)
    w_main = jnp.concatenate([mq, mv, mo, _split_pairs(rq), rv, rg, mk, _split_pairs(rk)], axis=1).astype(BF16)
    w_gates = jnp.pad(jnp.concatenate([mi, mf], axis=1), ((0, 0), (0, LANES - 4 * N_HEADS))).astype(BF16)
    return w_main, w_gates


def _trunk(x, ada, weights, tables):
    batch, seq, _ = x.shape
    (w_main, w_gates, bi_col, bf_col, norm_w, w_out, ln1_g, ln1_b, w_up, conv_w, conv_b, w_down,
     ln2_g, ln2_b) = weights
    sh1, sc1, g1, sh2, sc2, g2 = (t.reshape(batch, 1, D_MODEL) for t in jnp.split(ada, 6, axis=-1))
    x2 = x.reshape(batch * seq, D_MODEL)
    cos, sin = _rotary_tables(seq)
    n_chunks = seq // CHUNK

    z3, zk, gi, gf = _in_proj(x2, sc1, sh1, w_main, w_gates, cos, sin, seq)
    pack, arow, wsrow, decrow = _gates(gi, gf, jnp.tile(bi_col, (n_chunks, 1)), jnp.tile(bf_col, (n_chunks, 1)),
                                       batch, seq)
    ym = _mlstm(z3, zk, pack, arow, wsrow, decrow, norm_w, batch, seq)
    yr = _retention(z3, zk, *tables, batch, seq)
    x1 = _out_proj(ym, yr, x2, g1, w_out, ln1_g, ln1_b, seq)
    out = _ffn(x1, sc2, sh2, g2, w_up, conv_w, conv_b, w_down, ln2_g, ln2_b, seq)
    return out.reshape(batch, seq, D_MODEL)


def _prepare_weights(w_in, b_igate, b_fgate, mlstm_norm_w, w_out, ln1_g, ln1_b, w_up, conv_w, conv_b, w_down,
                     ln2_g, ln2_b):
    bcast_col = lambda b: jnp.broadcast_to(b[:, None], (2 * N_HEADS, CHUNK))
    return _layout_w_in(w_in[0]) + (
        bcast_col(b_igate[0]), bcast_col(b_fgate[0]),
        mlstm_norm_w[0].reshape(N_HEADS, 1, D_HEAD), w_out[0].astype(BF16), ln1_g, ln1_b,
        w_up[0].astype(BF16), conv_w[0], conv_b, w_down[0].astype(BF16), ln2_g, ln2_b,
    )


def kernel(x_prompt, x_sample, c_prompt, c_sample, w_ada, b_ada, w_in, b_igate, b_fgate, mlstm_norm_w, w_out,
           ln1_g, ln1_b, w_up, conv_w, conv_b, w_down, ln2_g, ln2_b):
    assert w_ada.shape[0] == DEPTH
    n_prompt, n_sample = c_prompt.shape[0], c_sample.shape[0]
    c_all = jnp.concatenate([c_prompt, c_sample], axis=0)
    pad_rows = -c_all.shape[0] % SUBLANES
    c_all = jnp.pad(c_all, ((0, pad_rows), (0, 0)))
    ada = _ada(c_all, w_ada[0], b_ada[0][None, :])

    weights = _prepare_weights(w_in, b_igate, b_fgate, mlstm_norm_w, w_out, ln1_g, ln1_b, w_up, conv_w, conv_b,
                               w_down, ln2_g, ln2_b)
    tables = _retention_tables()
    y_prompt = _trunk(x_prompt, ada[:n_prompt], weights, tables)
    y_sample = _trunk(x_sample, ada[n_prompt:n_prompt + n_sample], weights, tables)
    return (y_prompt, y_sample)
```

```python
import functools

import jax
import jax.numpy as jnp
import numpy as np
from jax import lax
from jax.experimental import pallas as pl
from jax.experimental.pallas import tpu as pltpu

F32 = jnp.float32
BF16 = jnp.bfloat16

D_MODEL = 2048
N_HEADS = 4
D_HEAD = 256
D_GROUP = N_HEADS * D_HEAD
CHUNK = 128
D_FF = 5632
DEPTH = 1
DEEPNORM_ALPHA = (2.0 * DEPTH) ** 0.25
LN_EPS = 1e-5
NORM_EPS = 1e-6
ROPE_BASE = 10000.0
RET_DECAY_EXP_FWD = 5.0
RET_DECAY_EXP_BWD = 5.5
NEG_INIT = -1e30
HEAD_SCALE = D_HEAD ** -0.5

SUBLANES = 8
LANES = 128
VMEM_LIMIT_BYTES = 56 * 1024 * 1024

ZB_MQ, ZB_MV, ZB_MO, ZB_RQ, ZB_RV, ZB_RG = (N_HEADS * i for i in range(6))
N_ZBLOCKS = 6 * N_HEADS
KB_M, KB_R = 0, N_HEADS
N_KBLOCKS = 2 * N_HEADS

PK_M, PK_E, PK_IW = 0, 2, 4
N_PACK_ROWS = 6


def _sigmoid(x):
    return 1.0 / (1.0 + jnp.exp(-x))


def _log_sigmoid(x):
    return jnp.minimum(x, 0.0) - jnp.log1p(jnp.exp(-jnp.abs(x)))


def _layernorm(v, g, b):
    mu = jnp.mean(v, axis=-1, keepdims=True)
    d = v - mu
    var = jnp.mean(d * d, axis=-1, keepdims=True)
    return d * lax.rsqrt(var + LN_EPS) * g + b


def _params(*semantics):
    return pltpu.CompilerParams(dimension_semantics=semantics, vmem_limit_bytes=VMEM_LIMIT_BYTES)


ADA_TN = 1024


def _ada_kernel(c_ref, w_ref, b_ref, o_ref):
    c = c_ref[...]
    s = (c * _sigmoid(c)).astype(BF16)
    o_ref[...] = jnp.dot(s, w_ref[...].astype(BF16), preferred_element_type=F32) + b_ref[...]


def _ada(c, w_ada, b_ada):
    rows = c.shape[0]
    n_out = w_ada.shape[1]
    return pl.pallas_call(
        _ada_kernel,
        out_shape=jax.ShapeDtypeStruct((rows, n_out), F32),
        grid=(n_out // ADA_TN,),
        in_specs=[
            pl.BlockSpec((rows, D_MODEL), lambda n: (0, 0)),
            pl.BlockSpec((D_MODEL, ADA_TN), lambda n: (0, n)),
            pl.BlockSpec((1, ADA_TN), lambda n: (0, n)),
        ],
        out_specs=pl.BlockSpec((rows, ADA_TN), lambda n: (0, n)),
        compiler_params=_params("arbitrary"),
        name="ada",
    )(c, w_ada, b_ada)


INPROJ_TM = 1024
INPROJ_TN = D_GROUP
CHUNKS_PER_TM = INPROJ_TM // CHUNK
N_ROW_STEPS = N_ZBLOCKS // N_HEADS
N_KEY_STEPS = N_KBLOCKS // N_HEADS


def _inproj_kernel(x_ref, sc_ref, sh_ref, w_ref, wg_ref, cos_ref, sin_ref, z_ref, zk_ref, gi_ref, gf_ref, h_ref):
    n = pl.program_id(1)
    half = D_HEAD // 2
    n_gates = 2 * N_HEADS

    @pl.when(n == 0)
    def _():
        hb = (x_ref[...] * (1.0 + sc_ref[...]) + sh_ref[...]).astype(BF16)
        h_ref[...] = hb
        g = jnp.dot(hb, wg_ref[...], preferred_element_type=F32)
        for c in range(CHUNKS_PER_TM):
            gt = g[c * CHUNK:(c + 1) * CHUNK, :].T
            gi_ref[c * SUBLANES:(c + 1) * SUBLANES, :] = gt[0:n_gates, :]
            gf_ref[c * SUBLANES:(c + 1) * SUBLANES, :] = gt[n_gates:2 * n_gates, :]

    def step(epilogue):
        h = h_ref[...]
        for c in range(N_HEADS):
            epilogue(c, jnp.dot(h, w_ref[:, c * D_HEAD:(c + 1) * D_HEAD], preferred_element_type=F32))

    def elementwise(fn):
        def epilogue(c, t):
            z_ref[c] = fn(t).astype(BF16)
        return epilogue

    def rotate(t):
        a = t[:, 0:half]
        b = t[:, half:D_HEAD]
        return a * cos_ref[...] - b * sin_ref[...], a * sin_ref[...] + b * cos_ref[...]

    def rotary(c, t):
        ra, rb = rotate(t)
        z_ref[c, :, 0:half] = ra.astype(BF16)
        z_ref[c, :, half:D_HEAD] = rb.astype(BF16)

    def keys(halves_fn):
        def epilogue(c, t):
            for lo, part in zip((0, half), halves_fn(t)):
                for cc in range(CHUNKS_PER_TM):
                    zk_ref[c, cc, lo:lo + half, :] = part[cc * CHUNK:(cc + 1) * CHUNK, :].T.astype(BF16)
        return epilogue

    nb = lambda zb: zb // N_HEADS
    kb = lambda k: N_ROW_STEPS + k // N_HEADS
    pl.when(n == nb(ZB_MQ))(lambda: step(elementwise(lambda t: t * HEAD_SCALE)))
    pl.when((n == nb(ZB_MV)) | (n == nb(ZB_RV)))(lambda: step(elementwise(lambda t: t)))
    pl.when(n == nb(ZB_MO))(lambda: step(elementwise(_sigmoid)))
    pl.when(n == nb(ZB_RG))(lambda: step(elementwise(lambda t: t * _sigmoid(t))))
    pl.when(n == nb(ZB_RQ))(lambda: step(rotary))
    pl.when(n == kb(KB_M))(lambda: step(keys(lambda t: (t[:, 0:half], t[:, half:D_HEAD]))))
    pl.when(n == kb(KB_R))(lambda: step(keys(lambda t: tuple(r * HEAD_SCALE for r in rotate(t)))))


def _in_proj(x2, sc, sh, w_main, wg, cos, sin, seq):
    tokens = x2.shape[0]
    tm, tn = INPROJ_TM, INPROJ_TN
    tiles_per_seq = seq // tm
    batch_of = lambda i: i // tiles_per_seq
    row_step = lambda n: jnp.minimum(n, N_ROW_STEPS - 1)
    key_step = lambda n: jnp.maximum(n - N_ROW_STEPS, 0)
    half = D_HEAD // 2
    return pl.pallas_call(
        _inproj_kernel,
        out_shape=(
            jax.ShapeDtypeStruct((N_ZBLOCKS, tokens, D_HEAD), BF16),
            jax.ShapeDtypeStruct((N_KBLOCKS, tokens // CHUNK, D_HEAD, CHUNK), BF16),
            jax.ShapeDtypeStruct((tokens // CHUNK * SUBLANES, CHUNK), F32),
            jax.ShapeDtypeStruct((tokens // CHUNK * SUBLANES, CHUNK), F32),
        ),
        grid=(tokens // tm, N_ROW_STEPS + N_KEY_STEPS),
        in_specs=[
            pl.BlockSpec((tm, D_MODEL), lambda i, n: (i, 0)),
            pl.BlockSpec((None, 1, D_MODEL), lambda i, n: (batch_of(i), 0, 0)),
            pl.BlockSpec((None, 1, D_MODEL), lambda i, n: (batch_of(i), 0, 0)),
            pl.BlockSpec((D_MODEL, tn), lambda i, n: (0, n)),
            pl.BlockSpec((D_MODEL, LANES), lambda i, n: (0, 0)),
            pl.BlockSpec((tm, half), lambda i, n: (i % tiles_per_seq, 0)),
            pl.BlockSpec((tm, half), lambda i, n: (i % tiles_per_seq, 0)),
        ],
        out_specs=(
            pl.BlockSpec((N_HEADS, tm, D_HEAD), lambda i, n: (row_step(n), i, 0)),
            pl.BlockSpec((N_HEADS, CHUNKS_PER_TM, D_HEAD, CHUNK), lambda i, n: (key_step(n), i, 0, 0)),
            pl.BlockSpec((CHUNKS_PER_TM * SUBLANES, CHUNK), lambda i, n: (i, 0)),
            pl.BlockSpec((CHUNKS_PER_TM * SUBLANES, CHUNK), lambda i, n: (i, 0)),
        ),
        scratch_shapes=[pltpu.VMEM((tm, D_MODEL), BF16)],
        compiler_params=_params("arbitrary", "arbitrary"),
        name="in_proj",
    )(x2, sc, sh, w_main, wg, cos, sin)


def _lane_scan(x, op, suffix, lane):
    k = 1
    while k < CHUNK:
        if suffix:
            shifted = pltpu.roll(x, CHUNK - k, axis=1)
            valid = lane < CHUNK - k
        else:
            shifted = pltpu.roll(x, k, axis=1)
            valid = lane >= k
        x = jnp.where(valid, op(x, shifted), x)
        k *= 2
    return x


def _lane_allreduce(x, op):
    k = 1
    while k < CHUNK:
        x = op(x, pltpu.roll(x, k, axis=1))
        k *= 2
    return x


def _gates_kernel(gi_ref, gf_ref, bi_ref, bf_ref, pack_ref, arow_ref, wsrow_ref, decrow_ref,
                  m_s, e_s, iw_s, amax_s, glast_s, mstf_s, mstb_s, *, n_chunks):
    rows = n_chunks * SUBLANES
    lane = lax.broadcasted_iota(jnp.int32, (rows, CHUNK), 1)
    is_bwd = (lax.broadcasted_iota(jnp.int32, (rows, CHUNK), 0) % SUBLANES) >= N_HEADS

    ig = gi_ref[...] + bi_ref[...]
    lf = _log_sigmoid(gf_ref[...] + bf_ref[...])
    g = jnp.where(is_bwd, _lane_scan(lf, jnp.add, True, lane), _lane_scan(lf, jnp.add, False, lane))
    a = ig - g
    cm = jnp.where(is_bwd, _lane_scan(a, jnp.maximum, True, lane), _lane_scan(a, jnp.maximum, False, lane))
    amax = _lane_allreduce(a, jnp.maximum)
    arow_ref[...] = a
    amax_s[...] = amax
    glast_s[...] = _lane_allreduce(lf, jnp.add)

    def rec(i, carry):
        mf, mb = carry
        rf = pl.ds(pl.multiple_of(i * SUBLANES, SUBLANES), SUBLANES)
        rb = pl.ds(pl.multiple_of((n_chunks - 1 - i) * SUBLANES, SUBLANES), SUBLANES)
        mstf_s[rf, :] = mf
        mstb_s[rb, :] = mb
        mf = glast_s[rf, :] + jnp.maximum(mf, amax_s[rf, :])
        mb = glast_s[rb, :] + jnp.maximum(mb, amax_s[rb, :])
        return mf, mb

    init = jnp.full((SUBLANES, CHUNK), NEG_INIT, F32)
    lax.fori_loop(0, n_chunks, rec, (init, init))

    mst = jnp.where(is_bwd, mstb_s[...], mstf_s[...])
    m_row = jnp.maximum(cm, mst)
    m_last = jnp.maximum(amax, mst)
    m_s[...] = m_row
    e_s[...] = jnp.exp(-(g + m_row))
    iw_s[...] = jnp.exp(mst - m_row)
    wsrow_ref[...] = jnp.exp(a - m_last)
    decrow_ref[...] = jnp.exp(mst - m_last)

    top_rows = SUBLANES
    assert N_PACK_ROWS <= top_rows
    top_row = lax.broadcasted_iota(jnp.int32, (top_rows, CHUNK), 0)
    pad = jnp.zeros((CHUNK - top_rows, CHUNK), F32)

    def emit(j, _):
        r0 = pl.multiple_of(j * SUBLANES, SUBLANES)
        tok = pl.ds(pl.multiple_of(j * CHUNK, CHUNK), CHUNK)
        quantities = [s[pl.ds(r0, SUBLANES), :] for s in (m_s, e_s, iw_s)]
        for h in range(N_HEADS):
            top = jnp.zeros((top_rows, CHUNK), F32)
            for qi, qt in enumerate(quantities):
                for direction in range(2):
                    src = N_HEADS * direction + h
                    top = jnp.where(top_row == 2 * qi + direction, qt[src:src + 1, :], top)
            tile = jnp.concatenate([top, pad], axis=0)
            pack_ref[h, tok, :] = tile.T
        return 0

    lax.fori_loop(0, n_chunks, emit, 0)


def _gates(gi, gf, bi_col, bf_col, batch, seq):
    n_chunks = seq // CHUNK
    rows = n_chunks * SUBLANES
    row_layout = jax.ShapeDtypeStruct((batch * rows, CHUNK), F32)
    per_batch = pl.BlockSpec((rows, CHUNK), lambda b: (b, 0))
    shared = pl.BlockSpec((rows, CHUNK), lambda b: (0, 0))
    return pl.pallas_call(
        functools.partial(_gates_kernel, n_chunks=n_chunks),
        out_shape=(jax.ShapeDtypeStruct((batch, N_HEADS, seq, LANES), F32), row_layout, row_layout, row_layout),
        grid=(batch,),
        in_specs=[per_batch, per_batch, shared, shared],
        out_specs=(pl.BlockSpec((None, N_HEADS, seq, LANES), lambda b: (b, 0, 0, 0)),
                   per_batch, per_batch, per_batch),
        scratch_shapes=[pltpu.VMEM((rows, CHUNK), F32) for _ in range(7)],
        compiler_params=_params("arbitrary"),
        name="gates",
    )(gi, gf, bi_col, bf_col)


def _causal_mask(direction):
    row = lax.broadcasted_iota(jnp.int32, (CHUNK, CHUNK), 0)
    col = lax.broadcasted_iota(jnp.int32, (CHUNK, CHUNK), 1)
    return (col <= row) if direction == 0 else (col >= row)


def _chunk_tokens(j):
    return pl.ds(pl.multiple_of(j * CHUNK, CHUNK), CHUNK)


D_STATE = D_HEAD + LANES


def _mlstm_kernel(q_ref, kt_ref, v_ref, o_ref, pack_ref, arow_ref, wsrow_ref, decrow_ref, nw_ref, y_ref,
                  st_s, cst_s, *, n_chunks):
    head = pl.program_id(1)
    ones = jnp.ones((CHUNK, LANES), BF16)

    def v_ext(j):
        return jnp.concatenate([v_ref[_chunk_tokens(j), :], ones], axis=1)

    def gate_row(direction, j):
        return pl.ds(j * SUBLANES + head + N_HEADS * direction, 1)

    st_s[...] = jnp.zeros_like(st_s)

    def state_step(direction, j):
        st = st_s[direction]
        cst_s[direction, j] = st.astype(BF16)
        kw = (kt_ref[j].astype(F32) * wsrow_ref[gate_row(direction, j), :]).astype(BF16)
        dec = decrow_ref[gate_row(direction, j), :]
        dec = jnp.concatenate([dec] * (D_STATE // LANES), axis=1)
        st_s[direction] = dec * st + jnp.dot(kw, v_ext(j), preferred_element_type=F32)

    def pass_a(i, _):
        state_step(0, i)
        state_step(1, n_chunks - 1 - i)
        return 0

    lax.fori_loop(0, n_chunks, pass_a, 0, unroll=4)

    def pass_b(j, _):
        tok = _chunk_tokens(j)
        q = q_ref[tok, :]
        vx = v_ext(j)
        s = jnp.dot(q, kt_ref[j], preferred_element_type=F32)
        pk = pack_ref[tok, :]
        hm = None
        for direction in range(2):
            col = lambda base: pk[:, base + direction:base + direction + 1]
            m_row, e, iw = col(PK_M), col(PK_E), col(PK_IW)
            a = arow_ref[gate_row(direction, j), :]
            sc = (s * jnp.where(_causal_mask(direction), jnp.exp(a - m_row), 0.0)).astype(BF16)
            tot = (jnp.dot(sc, vx, preferred_element_type=F32)
                   + iw * jnp.dot(q, cst_s[direction, j], preferred_element_type=F32))
            r = 1.0 / jnp.maximum(jnp.abs(tot[:, D_HEAD:]), e)
            h = tot[:, :D_HEAD] * jnp.concatenate([r] * (D_HEAD // LANES), axis=1)
            hm = h if hm is None else hm + h
        mu = jnp.mean(hm, axis=-1, keepdims=True)
        d = hm - mu
        var = jnp.mean(d * d, axis=-1, keepdims=True)
        y = d * lax.rsqrt(var + NORM_EPS) * nw_ref[...] * o_ref[tok, :].astype(F32)
        y_ref[tok, :] = y.astype(BF16)
        return 0

    lax.fori_loop(0, n_chunks, pass_b, 0, unroll=4)


def _mixer_specs(seq, n_chunks, row_blocks, key_block):
    zspec = lambda zb: pl.BlockSpec((None, seq, D_HEAD), lambda b, h: (zb + h, b, 0))
    kspec = pl.BlockSpec((None, n_chunks, D_HEAD, CHUNK), lambda b, h: (key_block + h, b, 0, 0))
    q_block, v_block, gate_block = row_blocks
    return [zspec(q_block), kspec, zspec(v_block), zspec(gate_block)]


def _mlstm(z3, zk, pack, arow, wsrow, decrow, norm_w, batch, seq):
    n_chunks = seq // CHUNK
    rows = pl.BlockSpec((n_chunks * SUBLANES, CHUNK), lambda b, h: (b, 0))
    return pl.pallas_call(
        functools.partial(_mlstm_kernel, n_chunks=n_chunks),
        out_shape=jax.ShapeDtypeStruct((N_HEADS, batch * seq, D_HEAD), BF16),
        grid=(batch, N_HEADS),
        in_specs=_mixer_specs(seq, n_chunks, (ZB_MQ, ZB_MV, ZB_MO), KB_M) + [
            pl.BlockSpec((None, None, seq, LANES), lambda b, h: (b, h, 0, 0)),
            rows, rows, rows,
            pl.BlockSpec((None, 1, D_HEAD), lambda b, h: (h, 0, 0)),
        ],
        out_specs=pl.BlockSpec((None, seq, D_HEAD), lambda b, h: (h, b, 0)),
        scratch_shapes=[
            pltpu.VMEM((2, D_HEAD, D_STATE), F32),
            pltpu.VMEM((2, n_chunks, D_HEAD, D_STATE), BF16),
        ],
        compiler_params=_params("arbitrary", "arbitrary"),
        name="mlstm",
    )(z3, zk, z3, z3, pack, arow, wsrow, decrow, norm_w)


def _ret_kernel(q_ref, kt_ref, v_ref, g_ref, dsum_ref, xi_ref, zeta_ref, cd_ref, y_ref, st_s, rst_s, *, n_chunks):
    st_s[...] = jnp.zeros_like(st_s)

    def state_step(direction, j):
        st = st_s[direction]
        rst_s[direction, j] = st.astype(BF16)
        kz = (kt_ref[j].astype(F32) * zeta_ref[direction]).astype(BF16)
        cd = jnp.concatenate([cd_ref[direction]] * (D_HEAD // LANES), axis=1)
        st_s[direction] = cd * st + jnp.dot(kz, v_ref[_chunk_tokens(j), :], preferred_element_type=F32)

    def pass_a(i, _):
        state_step(0, i)
        state_step(1, n_chunks - 1 - i)
        return 0

    lax.fori_loop(0, n_chunks, pass_a, 0, unroll=4)

    def pass_b(j, _):
        tok = _chunk_tokens(j)
        q = q_ref[tok, :]
        sc = (jnp.dot(q, kt_ref[j], preferred_element_type=F32) * dsum_ref[...]).astype(BF16)
        yr = jnp.dot(sc, v_ref[tok, :], preferred_element_type=F32)
        for direction in range(2):
            yr = yr + xi_ref[direction] * jnp.dot(q, rst_s[direction, j], preferred_element_type=F32)
        yr = yr * lax.rsqrt(jnp.mean(yr * yr, axis=-1, keepdims=True) + NORM_EPS)
        y_ref[tok, :] = (yr * g_ref[tok, :].astype(F32)).astype(BF16)
        return 0

    lax.fori_loop(0, n_chunks, pass_b, 0, unroll=8)


def _retention(z3, zk, dsum, xi, zeta, cd, batch, seq):
    n_chunks = seq // CHUNK
    per_head = lambda *tail: pl.BlockSpec((None,) + tail, lambda b, h: (h,) + (0,) * len(tail))
    return pl.pallas_call(
        functools.partial(_ret_kernel, n_chunks=n_chunks),
        out_shape=jax.ShapeDtypeStruct((N_HEADS, batch * seq, D_HEAD), BF16),
        grid=(batch, N_HEADS),
        in_specs=_mixer_specs(seq, n_chunks, (ZB_RQ, ZB_RV, ZB_RG), KB_R) + [
            per_head(CHUNK, CHUNK), per_head(2, CHUNK, D_HEAD), per_head(2, 1, CHUNK), per_head(2, 1, LANES),
        ],
        out_specs=pl.BlockSpec((None, seq, D_HEAD), lambda b, h: (h, b, 0)),
        scratch_shapes=[
            pltpu.VMEM((2, D_HEAD, D_HEAD), F32),
            pltpu.VMEM((2, n_chunks, D_HEAD, D_HEAD), BF16),
        ],
        compiler_params=_params("arbitrary", "arbitrary"),
        name="retention",
    )(z3, zk, z3, z3, dsum, xi, zeta, cd)


OUTPROJ_TM = 512


def _outproj_kernel(ym_ref, yr_ref, x_ref, g_ref, w_ref, lg_ref, lb_ref, o_ref, y_s):
    for grp, y_ref in enumerate((ym_ref, yr_ref)):
        for c in range(N_HEADS):
            c0 = grp * D_GROUP + c * D_HEAD
            y_s[:, c0:c0 + D_HEAD] = y_ref[c]
    acc = jnp.dot(y_s[...], w_ref[...], preferred_element_type=F32)
    v = DEEPNORM_ALPHA * x_ref[...] + g_ref[...] * acc
    o_ref[...] = _layernorm(v, lg_ref[...], lb_ref[...])


def _out_proj(ym, yr, x2, g1, w_out, ln_g, ln_b, seq):
    tokens = x2.shape[0]
    tm = OUTPROJ_TM
    tiles_per_seq = seq // tm
    row = pl.BlockSpec((1, D_MODEL), lambda i: (0, 0))
    return pl.pallas_call(
        _outproj_kernel,
        out_shape=jax.ShapeDtypeStruct((tokens, D_MODEL), F32),
        grid=(tokens // tm,),
        in_specs=[
            pl.BlockSpec((N_HEADS, tm, D_HEAD), lambda i: (0, i, 0)),
            pl.BlockSpec((N_HEADS, tm, D_HEAD), lambda i: (0, i, 0)),
            pl.BlockSpec((tm, D_MODEL), lambda i: (i, 0)),
            pl.BlockSpec((None, 1, D_MODEL), lambda i: (i // tiles_per_seq, 0, 0)),
            pl.BlockSpec((D_MODEL, D_MODEL), lambda i: (0, 0)),
            row, row,
        ],
        out_specs=pl.BlockSpec((tm, D_MODEL), lambda i: (i, 0)),
        scratch_shapes=[pltpu.VMEM((tm, D_MODEL), BF16)],
        compiler_params=_params("arbitrary"),
        name="out_proj",
    )(ym, yr, x2, g1, w_out, ln_g, ln_b)


FFN_TM = 512
FFN_TF = 512
FFN_BLOCKS_PER_STEP = 2
HALO = SUBLANES
N_BLOCK_REFS = 7


def _ffn_kernel(x_ref, xp_ref, xn_ref, sc_ref, sh_ref, g_ref, *refs, tiles_per_seq, n_steps, blocks_in_last):
    n_w = N_BLOCK_REFS * FFN_BLOCKS_PER_STEP
    block_refs = [refs[k * N_BLOCK_REFS:(k + 1) * N_BLOCK_REFS] for k in range(FFN_BLOCKS_PER_STEP)]
    lg_ref, lb_ref, o_ref, h_s, acc_s = refs[n_w:]
    i = pl.program_id(0)
    f = pl.program_id(1)
    tm = FFN_TM

    @pl.when(f == 0)
    def _():
        scale = 1.0 + sc_ref[...]
        shift = sh_ref[...]
        has_prev = (i % tiles_per_seq != 0).astype(F32)
        has_next = (i % tiles_per_seq != tiles_per_seq - 1).astype(F32)
        h_s[0:HALO, :] = ((xp_ref[...] * scale + shift) * has_prev).astype(BF16)
        h_s[HALO:HALO + tm, :] = (x_ref[...] * scale + shift).astype(BF16)
        h_s[HALO + tm:, :] = ((xn_ref[...] * scale + shift) * has_next).astype(BF16)
        acc_s[...] = jnp.zeros_like(acc_s)

    def run_blocks(n_blocks):
        h = h_s[...]
        update = None
        for wa_ref, wg_ref, cwa_ref, cwg_ref, cba_ref, cbg_ref, wd_ref in block_refs[:n_blocks]:
            def conv_branch(w_ref, cw_ref, cb_ref):
                u = jnp.dot(h, w_ref[...], preferred_element_type=F32)
                cw = cw_ref[...]
                return (cw[0:1, :] * u[HALO - 1:HALO - 1 + tm, :] + cw[1:2, :] * u[HALO:HALO + tm, :]
                        + cw[2:3, :] * u[HALO + 1:HALO + 1 + tm, :] + cb_ref[...])

            a = conv_branch(wa_ref, cwa_ref, cba_ref)
            g = conv_branch(wg_ref, cwg_ref, cbg_ref)
            act = (a * _sigmoid(a) * g).astype(BF16)
            contrib = jnp.dot(act, wd_ref[...], preferred_element_type=F32)
            update = contrib if update is None else update + contrib
        acc_s[...] += update

    if blocks_in_last == FFN_BLOCKS_PER_STEP:
        run_blocks(FFN_BLOCKS_PER_STEP)
    else:
        pl.when(f < n_steps - 1)(lambda: run_blocks(FFN_BLOCKS_PER_STEP))
        pl.when(f == n_steps - 1)(lambda: run_blocks(blocks_in_last))

    @pl.when(f == n_steps - 1)
    def _():
        v = DEEPNORM_ALPHA * x_ref[...] + g_ref[...] * acc_s[...]
        o_ref[...] = _layernorm(v, lg_ref[...], lb_ref[...])


def _ffn(x1, sc2, sh2, g2, w_up, conv_w, conv_b, w_down, ln_g, ln_b, seq):
    tokens = x1.shape[0]
    tm, tf = FFN_TM, FFN_TF
    tiles_per_seq = seq // tm
    n_f = D_FF // tf
    n_steps = pl.cdiv(n_f, FFN_BLOCKS_PER_STEP)
    blocks_in_last = n_f - (n_steps - 1) * FFN_BLOCKS_PER_STEP
    halo_per_tile = tm // HALO
    n_halo_blocks = tokens // HALO
    mod = pl.BlockSpec((None, 1, D_MODEL), lambda i, f: (i // tiles_per_seq, 0, 0))
    row = pl.BlockSpec((1, D_MODEL), lambda i, f: (0, 0))

    block_specs, block_args = [], []
    for k in range(FFN_BLOCKS_PER_STEP):
        blk = lambda f, k=k: jnp.minimum(f * FFN_BLOCKS_PER_STEP + k, n_f - 1)
        block_specs += [
            pl.BlockSpec((D_MODEL, tf), lambda i, f, blk=blk: (0, blk(f))),
            pl.BlockSpec((D_MODEL, tf), lambda i, f, blk=blk: (0, n_f + blk(f))),
            pl.BlockSpec((3, tf), lambda i, f, blk=blk: (0, blk(f))),
            pl.BlockSpec((3, tf), lambda i, f, blk=blk: (0, n_f + blk(f))),
            pl.BlockSpec((1, tf), lambda i, f, blk=blk: (0, blk(f))),
            pl.BlockSpec((1, tf), lambda i, f, blk=blk: (0, n_f + blk(f))),
            pl.BlockSpec((tf, D_MODEL), lambda i, f, blk=blk: (blk(f), 0)),
        ]
        block_args += [w_up, w_up, conv_w, conv_w, conv_b, conv_b, w_down]

    return pl.pallas_call(
        functools.partial(_ffn_kernel, tiles_per_seq=tiles_per_seq, n_steps=n_steps, blocks_in_last=blocks_in_last),
        out_shape=jax.ShapeDtypeStruct((tokens, D_MODEL), F32),
        grid=(tokens // tm, n_steps),
        in_specs=[
            pl.BlockSpec((tm, D_MODEL), lambda i, f: (i, 0)),
            pl.BlockSpec((HALO, D_MODEL), lambda i, f: (jnp.maximum(i * halo_per_tile - 1, 0), 0)),
            pl.BlockSpec((HALO, D_MODEL),
                         lambda i, f: (jnp.minimum((i + 1) * halo_per_tile, n_halo_blocks - 1), 0)),
            mod, mod, mod,
        ] + block_specs + [row, row],
        out_specs=pl.BlockSpec((tm, D_MODEL), lambda i, f: (i, 0)),
        scratch_shapes=[
            pltpu.VMEM((tm + 2 * HALO, D_MODEL), BF16),
            pltpu.VMEM((tm, D_MODEL), F32),
        ],
        compiler_params=_params("arbitrary", "arbitrary"),
        name="ffn",
    )(x1, x1, x1, sc2, sh2, g2, *block_args, ln_g, ln_b)


def _rotary_tables(seq):
    half = D_HEAD // 2
    inv = 1.0 / (ROPE_BASE ** jnp.linspace(0.0, 1.0, half, dtype=F32))
    ang = jnp.arange(seq, dtype=F32)[:, None] * inv[None, :]
    return jnp.cos(ang), jnp.sin(ang)


def _retention_tables():
    hd = jnp.arange(N_HEADS, dtype=F32)
    lg = jnp.stack([jnp.log1p(-jnp.exp2(-RET_DECAY_EXP_FWD - hd)),
                    jnp.log1p(-jnp.exp2(-RET_DECAY_EXP_BWD - hd))], axis=1)
    pos = jnp.arange(CHUNK, dtype=F32)
    diff = pos[:, None] - pos[None, :]
    dmat = jnp.where(diff >= 0, jnp.exp(lg[:, :, None, None] * jnp.maximum(diff, 0.0)), 0.0)
    xi = jnp.exp(lg[:, :, None] * (pos + 1.0))
    zeta = jnp.exp(lg[:, :, None] * (CHUNK - 1.0 - pos))
    cd = jnp.exp(lg * CHUNK)
    dsum = dmat[:, 0] + dmat[:, 1, ::-1, ::-1]
    xi = jnp.stack([xi[:, 0], xi[:, 1, ::-1]], axis=1)
    zeta = jnp.stack([zeta[:, 0], zeta[:, 1, ::-1]], axis=1)
    xi = jnp.broadcast_to(xi[..., None], xi.shape + (D_HEAD,))
    zeta = zeta[:, :, None, :]
    cd = jnp.broadcast_to(cd[:, :, None, None], (N_HEADS, 2, 1, LANES))
    return dsum, xi, zeta, cd


def _split_pairs(w):
    w = w.reshape(D_MODEL, N_HEADS, D_HEAD // 2, 2)
    return jnp.concatenate([w[..., 0], w[..., 1]], axis=-1).reshape(D_MODEL, D_GROUP)


def _layout_w_in(w_in):
    sizes = [D_GROUP] * 4 + [2 * N_HEADS] * 2 + [D_GROUP] * 4
    mq, mk, mv, mo, mi, mf, rq, rk, rv, rg = jnp.split(w_in, [int(s) for s in np.cumsum(sizes)[:-1]], axis=1)
    w_main = jnp.concatenate([mq, mv, mo, _split_pairs(rq), rv, rg, mk, _split_pairs(rk)], axis=1).astype(BF16)
    w_gates = jnp.pad(jnp.concatenate([mi, mf], axis=1), ((0, 0), (0, LANES - 4 * N_HEADS))).astype(BF16)
    return w_main, w_gates


def _trunk(x, ada, weights, tables):
    batch, seq, _ = x.shape
    (w_main, w_gates, bi_col, bf_col, norm_w, w_out, ln1_g, ln1_b, w_up, conv_w, conv_b, w_down,
     ln2_g, ln2_b) = weights
    sh1, sc1, g1, sh2, sc2, g2 = (t.reshape(batch, 1, D_MODEL) for t in jnp.split(ada, 6, axis=-1))
    x2 = x.reshape(batch * seq, D_MODEL)
    cos, sin = _rotary_tables(seq)
    n_chunks = seq // CHUNK

    z3, zk, gi, gf = _in_proj(x2, sc1, sh1, w_main, w_gates, cos, sin, seq)
    pack, arow, wsrow, decrow = _gates(gi, gf, jnp.tile(bi_col, (n_chunks, 1)), jnp.tile(bf_col, (n_chunks, 1)),
                                       batch, seq)
    ym = _mlstm(z3, zk, pack, arow, wsrow, decrow, norm_w, batch, seq)
    yr = _retention(z3, zk, *tables, batch, seq)
    x1 = _out_proj(ym, yr, x2, g1, w_out, ln1_g, ln1_b, seq)
    out = _ffn(x1, sc2, sh2, g2, w_up, conv_w, conv_b, w_down, ln2_g, ln2_b, seq)
    return out.reshape(batch, seq, D_MODEL)


def _prepare_weights(w_in, b_igate, b_fgate, mlstm_norm_w, w_out, ln1_g, ln1_b, w_up, conv_w, conv_b, w_down,
                     ln2_g, ln2_b):
    bcast_col = lambda b: jnp.broadcast_to(b[:, None], (2 * N_HEADS, CHUNK))
    return _layout_w_in(w_in[0]) + (
        bcast_col(b_igate[0]), bcast_col(b_fgate[0]),
        mlstm_norm_w[0].reshape(N_HEADS, 1, D_HEAD), w_out[0].astype(BF16), ln1_g, ln1_b,
        w_up[0].astype(BF16), conv_w[0], conv_b, w_down[0].astype(BF16), ln2_g, ln2_b,
    )


def kernel(x_prompt, x_sample, c_prompt, c_sample, w_ada, b_ada, w_in, b_igate, b_fgate, mlstm_norm_w, w_out,
           ln1_g, ln1_b, w_up, conv_w, conv_b, w_down, ln2_g, ln2_b):
    assert w_ada.shape[0] == DEPTH
    n_prompt, n_sample = c_prompt.shape[0], c_sample.shape[0]
    c_all = jnp.concatenate([c_prompt, c_sample], axis=0)
    pad_rows = -c_all.shape[0] % SUBLANES
    c_all = jnp.pad(c_all, ((0, pad_rows), (0, 0)))
    ada = _ada(c_all, w_ada[0], b_ada[0][None, :])

    weights = _prepare_weights(w_in, b_igate, b_fgate, mlstm_norm_w, w_out, ln1_g, ln1_b, w_up, conv_w, conv_b,
                               w_down, ln2_g, ln2_b)
    tables = _retention_tables()
    y_prompt = _trunk(x_prompt, ada[:n_prompt], weights, tables)
    y_sample = _trunk(x_sample, ada[n_prompt:n_prompt + n_sample], weights, tables)
    return (y_prompt, y_sample)
```

```python
import functools

import jax
import jax.numpy as jnp
import numpy as np
from jax import lax
from jax.experimental import pallas as pl
from jax.experimental.pallas import tpu as pltpu

F32 = jnp.float32
BF16 = jnp.bfloat16

D_MODEL = 2048
N_HEADS = 4
D_HEAD = 256
D_GROUP = N_HEADS * D_HEAD
CHUNK = 128
D_FF = 5632
DEPTH = 1
DEEPNORM_ALPHA = (2.0 * DEPTH) ** 0.25
LN_EPS = 1e-5
NORM_EPS = 1e-6
ROPE_BASE = 10000.0
RET_DECAY_EXP_FWD = 5.0
RET_DECAY_EXP_BWD = 5.5
NEG_INIT = -1e30
HEAD_SCALE = D_HEAD ** -0.5

SUBLANES = 8
LANES = 128
VMEM_LIMIT_BYTES = 56 * 1024 * 1024

ADA_SH1, ADA_SC1, ADA_G1, ADA_SH2, ADA_SC2, ADA_G2 = range(6)

ZB_MQ, ZB_MV, ZB_MO, ZB_RQ, ZB_RV, ZB_RG = (N_HEADS * i for i in range(6))
N_ZBLOCKS = 6 * N_HEADS
KB_M, KB_R = 0, N_HEADS
N_KBLOCKS = 2 * N_HEADS

PK_M, PK_E, PK_IW = 0, 2, 4
N_PACK_ROWS = 6


def _sigmoid(x):
    return 1.0 / (1.0 + jnp.exp(-x))


def _log_sigmoid(x):
    return jnp.minimum(x, 0.0) - jnp.log1p(jnp.exp(-jnp.abs(x)))


def _layernorm(v, g, b):
    mu = jnp.mean(v, axis=-1, keepdims=True)
    d = v - mu
    var = jnp.mean(d * d, axis=-1, keepdims=True)
    return d * lax.rsqrt(var + LN_EPS) * g + b


def _params(*semantics):
    return pltpu.CompilerParams(dimension_semantics=semantics, vmem_limit_bytes=VMEM_LIMIT_BYTES)


ADA_TN = 1024


def _ada_kernel(c_ref, w_ref, b_ref, o_ref):
    c = c_ref[...]
    s = (c * _sigmoid(c)).astype(BF16)
    o_ref[...] = jnp.dot(s, w_ref[...].astype(BF16), preferred_element_type=F32) + b_ref[...]


def _ada(c, w_ada, b_ada):
    rows = c.shape[0]
    n_out = w_ada.shape[1]
    per_vec = D_MODEL // ADA_TN
    return pl.pallas_call(
        _ada_kernel,
        out_shape=jax.ShapeDtypeStruct((n_out // D_MODEL, rows, D_MODEL), F32),
        grid=(n_out // ADA_TN,),
        in_specs=[
            pl.BlockSpec((rows, D_MODEL), lambda n: (0, 0)),
            pl.BlockSpec((D_MODEL, ADA_TN), lambda n: (0, n)),
            pl.BlockSpec((1, ADA_TN), lambda n: (0, n)),
        ],
        out_specs=pl.BlockSpec((None, rows, ADA_TN), lambda n: (n // per_vec, 0, n % per_vec)),
        compiler_params=_params("arbitrary"),
        name="ada",
    )(c, w_ada, b_ada)


INPROJ_TM = 1024
INPROJ_TN = D_GROUP
CHUNKS_PER_TM = INPROJ_TM // CHUNK
N_ROW_STEPS = N_ZBLOCKS // N_HEADS
N_KEY_STEPS = N_KBLOCKS // N_HEADS


def _inproj_kernel(x_ref, sc_ref, sh_ref, w_ref, wg_ref, cos_ref, sin_ref, z_ref, zk_ref, gi_ref, gf_ref, h_ref):
    n = pl.program_id(1)
    half = D_HEAD // 2
    n_gates = 2 * N_HEADS

    @pl.when(n == 0)
    def _():
        hb = (x_ref[...] * (1.0 + sc_ref[...]) + sh_ref[...]).astype(BF16)
        h_ref[...] = hb
        g = jnp.dot(hb, wg_ref[...], preferred_element_type=F32)
        for c in range(CHUNKS_PER_TM):
            gt = g[c * CHUNK:(c + 1) * CHUNK, :].T
            gi_ref[c * SUBLANES:(c + 1) * SUBLANES, :] = gt[0:n_gates, :]
            gf_ref[c * SUBLANES:(c + 1) * SUBLANES, :] = gt[n_gates:2 * n_gates, :]

    def step(epilogue):
        h = h_ref[...]
        for c in range(N_HEADS):
            epilogue(c, jnp.dot(h, w_ref[:, c * D_HEAD:(c + 1) * D_HEAD], preferred_element_type=F32))

    def elementwise(fn):
        def epilogue(c, t):
            z_ref[c] = fn(t).astype(BF16)
        return epilogue

    def rotate(t):
        a = t[:, 0:half]
        b = t[:, half:D_HEAD]
        return a * cos_ref[...] - b * sin_ref[...], a * sin_ref[...] + b * cos_ref[...]

    def rotary(c, t):
        ra, rb = rotate(t)
        z_ref[c, :, 0:half] = ra.astype(BF16)
        z_ref[c, :, half:D_HEAD] = rb.astype(BF16)

    def keys(halves_fn):
        def epilogue(c, t):
            for lo, part in zip((0, half), halves_fn(t)):
                for cc in range(CHUNKS_PER_TM):
                    zk_ref[c, cc, lo:lo + half, :] = part[cc * CHUNK:(cc + 1) * CHUNK, :].T.astype(BF16)
        return epilogue

    nb = lambda zb: zb // N_HEADS
    kb = lambda k: N_ROW_STEPS + k // N_HEADS
    pl.when(n == nb(ZB_MQ))(lambda: step(elementwise(lambda t: t * HEAD_SCALE)))
    pl.when((n == nb(ZB_MV)) | (n == nb(ZB_RV)))(lambda: step(elementwise(lambda t: t)))
    pl.when(n == nb(ZB_MO))(lambda: step(elementwise(_sigmoid)))
    pl.when(n == nb(ZB_RG))(lambda: step(elementwise(lambda t: t * _sigmoid(t))))
    pl.when(n == nb(ZB_RQ))(lambda: step(rotary))
    pl.when(n == kb(KB_M))(lambda: step(keys(lambda t: (t[:, 0:half], t[:, half:D_HEAD]))))
    pl.when(n == kb(KB_R))(lambda: step(keys(lambda t: tuple(r * HEAD_SCALE for r in rotate(t)))))


def _in_proj(x2, ada, mod_spec, w_main, wg, cos, sin, seq):
    tokens = x2.shape[0]
    tm, tn = INPROJ_TM, INPROJ_TN
    tiles_per_seq = seq // tm
    row_step = lambda n: jnp.minimum(n, N_ROW_STEPS - 1)
    key_step = lambda n: jnp.maximum(n - N_ROW_STEPS, 0)
    half = D_HEAD // 2
    return pl.pallas_call(
        _inproj_kernel,
        out_shape=(
            jax.ShapeDtypeStruct((N_ZBLOCKS, tokens, D_HEAD), BF16),
            jax.ShapeDtypeStruct((N_KBLOCKS, tokens // CHUNK, D_HEAD, CHUNK), BF16),
            jax.ShapeDtypeStruct((tokens // CHUNK * SUBLANES, CHUNK), F32),
            jax.ShapeDtypeStruct((tokens // CHUNK * SUBLANES, CHUNK), F32),
        ),
        grid=(tokens // tm, N_ROW_STEPS + N_KEY_STEPS),
        in_specs=[
            pl.BlockSpec((tm, D_MODEL), lambda i, n: (i, 0)),
            mod_spec(ADA_SC1, tiles_per_seq), mod_spec(ADA_SH1, tiles_per_seq),
            pl.BlockSpec((D_MODEL, tn), lambda i, n: (0, n)),
            pl.BlockSpec((D_MODEL, LANES), lambda i, n: (0, 0)),
            pl.BlockSpec((tm, half), lambda i, n: (i % tiles_per_seq, 0)),
            pl.BlockSpec((tm, half), lambda i, n: (i % tiles_per_seq, 0)),
        ],
        out_specs=(
            pl.BlockSpec((N_HEADS, tm, D_HEAD), lambda i, n: (row_step(n), i, 0)),
            pl.BlockSpec((N_HEADS, CHUNKS_PER_TM, D_HEAD, CHUNK), lambda i, n: (key_step(n), i, 0, 0)),
            pl.BlockSpec((CHUNKS_PER_TM * SUBLANES, CHUNK), lambda i, n: (i, 0)),
            pl.BlockSpec((CHUNKS_PER_TM * SUBLANES, CHUNK), lambda i, n: (i, 0)),
        ),
        scratch_shapes=[pltpu.VMEM((tm, D_MODEL), BF16)],
        compiler_params=_params("arbitrary", "arbitrary"),
        name="in_proj",
    )(x2, ada, ada, w_main, wg, cos, sin)


def _lane_scan(x, op, suffix, lane):
    k = 1
    while k < CHUNK:
        if suffix:
            shifted = pltpu.roll(x, CHUNK - k, axis=1)
            valid = lane < CHUNK - k
        else:
            shifted = pltpu.roll(x, k, axis=1)
            valid = lane >= k
        x = jnp.where(valid, op(x, shifted), x)
        k *= 2
    return x


def _lane_allreduce(x, op):
    k = 1
    while k < CHUNK:
        x = op(x, pltpu.roll(x, k, axis=1))
        k *= 2
    return x


def _gates_kernel(gi_ref, gf_ref, bi_ref, bf_ref, pack_ref, arow_ref, wsrow_ref, decrow_ref,
                  m_s, e_s, iw_s, amax_s, glast_s, mstf_s, mstb_s, *, n_chunks):
    rows = n_chunks * SUBLANES
    lane = lax.broadcasted_iota(jnp.int32, (rows, CHUNK), 1)
    is_bwd = (lax.broadcasted_iota(jnp.int32, (rows, CHUNK), 0) % SUBLANES) >= N_HEADS

    ig = gi_ref[...] + bi_ref[...]
    lf = _log_sigmoid(gf_ref[...] + bf_ref[...])
    g = jnp.where(is_bwd, _lane_scan(lf, jnp.add, True, lane), _lane_scan(lf, jnp.add, False, lane))
    a = ig - g
    cm = jnp.where(is_bwd, _lane_scan(a, jnp.maximum, True, lane), _lane_scan(a, jnp.maximum, False, lane))
    amax = _lane_allreduce(a, jnp.maximum)
    arow_ref[...] = a
    amax_s[...] = amax
    glast_s[...] = _lane_allreduce(lf, jnp.add)

    def rec(i, carry):
        mf, mb = carry
        rf = pl.ds(pl.multiple_of(i * SUBLANES, SUBLANES), SUBLANES)
        rb = pl.ds(pl.multiple_of((n_chunks - 1 - i) * SUBLANES, SUBLANES), SUBLANES)
        mstf_s[rf, :] = mf
        mstb_s[rb, :] = mb
        mf = glast_s[rf, :] + jnp.maximum(mf, amax_s[rf, :])
        mb = glast_s[rb, :] + jnp.maximum(mb, amax_s[rb, :])
        return mf, mb

    init = jnp.full((SUBLANES, CHUNK), NEG_INIT, F32)
    lax.fori_loop(0, n_chunks, rec, (init, init))

    mst = jnp.where(is_bwd, mstb_s[...], mstf_s[...])
    m_row = jnp.maximum(cm, mst)
    m_last = jnp.maximum(amax, mst)
    m_s[...] = m_row
    e_s[...] = jnp.exp(-(g + m_row))
    iw_s[...] = jnp.exp(mst - m_row)
    wsrow_ref[...] = jnp.exp(a - m_last)
    decrow_ref[...] = jnp.exp(mst - m_last)

    top_rows = SUBLANES
    assert N_PACK_ROWS <= top_rows
    top_row = lax.broadcasted_iota(jnp.int32, (top_rows, CHUNK), 0)
    pad = jnp.zeros((CHUNK - top_rows, CHUNK), F32)

    def emit(j, _):
        r0 = pl.multiple_of(j * SUBLANES, SUBLANES)
        tok = pl.ds(pl.multiple_of(j * CHUNK, CHUNK), CHUNK)
        quantities = [s[pl.ds(r0, SUBLANES), :] for s in (m_s, e_s, iw_s)]
        for h in range(N_HEADS):
            top = jnp.zeros((top_rows, CHUNK), F32)
            for qi, qt in enumerate(quantities):
                for direction in range(2):
                    src = N_HEADS * direction + h
                    top = jnp.where(top_row == 2 * qi + direction, qt[src:src + 1, :], top)
            tile = jnp.concatenate([top, pad], axis=0)
            pack_ref[h, tok, :] = tile.T
        return 0

    lax.fori_loop(0, n_chunks, emit, 0)


def _gates(gi, gf, bi_col, bf_col, batch, seq):
    n_chunks = seq // CHUNK
    rows = n_chunks * SUBLANES
    row_layout = jax.ShapeDtypeStruct((batch * rows, CHUNK), F32)
    per_batch = pl.BlockSpec((rows, CHUNK), lambda b: (b, 0))
    shared = pl.BlockSpec((rows, CHUNK), lambda b: (0, 0))
    return pl.pallas_call(
        functools.partial(_gates_kernel, n_chunks=n_chunks),
        out_shape=(jax.ShapeDtypeStruct((batch, N_HEADS, seq, LANES), F32), row_layout, row_layout, row_layout),
        grid=(batch,),
        in_specs=[per_batch, per_batch, shared, shared],
        out_specs=(pl.BlockSpec((None, N_HEADS, seq, LANES), lambda b: (b, 0, 0, 0)),
                   per_batch, per_batch, per_batch),
        scratch_shapes=[pltpu.VMEM((rows, CHUNK), F32) for _ in range(7)],
        compiler_params=_params("arbitrary"),
        name="gates",
    )(gi, gf, bi_col, bf_col)


def _causal_mask(direction):
    row = lax.broadcasted_iota(jnp.int32, (CHUNK, CHUNK), 0)
    col = lax.broadcasted_iota(jnp.int32, (CHUNK, CHUNK), 1)
    return (col <= row) if direction == 0 else (col >= row)


def _chunk_tokens(j):
    return pl.ds(pl.multiple_of(j * CHUNK, CHUNK), CHUNK)


D_STATE = D_HEAD + LANES


def _mlstm_kernel(q_ref, kt_ref, v_ref, o_ref, pack_ref, arow_ref, wsrow_ref, decrow_ref, nw_ref, y_ref,
                  st_s, cst_s, *, n_chunks):
    head = pl.program_id(1)
    ones = jnp.ones((CHUNK, LANES), BF16)

    def v_ext(j):
        return jnp.concatenate([v_ref[_chunk_tokens(j), :], ones], axis=1)

    def gate_row(direction, j):
        return pl.ds(j * SUBLANES + head + N_HEADS * direction, 1)

    st_s[...] = jnp.zeros_like(st_s)

    def state_step(direction, j):
        st = st_s[direction]
        cst_s[direction, j] = st.astype(BF16)
        kw = (kt_ref[j].astype(F32) * wsrow_ref[gate_row(direction, j), :]).astype(BF16)
        dec = decrow_ref[gate_row(direction, j), :]
        dec = jnp.concatenate([dec] * (D_STATE // LANES), axis=1)
        st_s[direction] = dec * st + jnp.dot(kw, v_ext(j), preferred_element_type=F32)

    def pass_a(i, _):
        state_step(0, i)
        state_step(1, n_chunks - 1 - i)
        return 0

    lax.fori_loop(0, n_chunks, pass_a, 0, unroll=4)

    def pass_b(j, _):
        tok = _chunk_tokens(j)
        q = q_ref[tok, :]
        vx = v_ext(j)
        s = jnp.dot(q, kt_ref[j], preferred_element_type=F32)
        pk = pack_ref[tok, :]
        hm = None
        for direction in range(2):
            col = lambda base: pk[:, base + direction:base + direction + 1]
            m_row, e, iw = col(PK_M), col(PK_E), col(PK_IW)
            a = arow_ref[gate_row(direction, j), :]
            sc = (s * jnp.where(_causal_mask(direction), jnp.exp(a - m_row), 0.0)).astype(BF16)
            tot = (jnp.dot(sc, vx, preferred_element_type=F32)
                   + iw * jnp.dot(q, cst_s[direction, j], preferred_element_type=F32))
            r = 1.0 / jnp.maximum(jnp.abs(tot[:, D_HEAD:]), e)
            h = tot[:, :D_HEAD] * jnp.concatenate([r] * (D_HEAD // LANES), axis=1)
            hm = h if hm is None else hm + h
        mu = jnp.mean(hm, axis=-1, keepdims=True)
        d = hm - mu
        var = jnp.mean(d * d, axis=-1, keepdims=True)
        y = d * lax.rsqrt(var + NORM_EPS) * nw_ref[...] * o_ref[tok, :].astype(F32)
        y_ref[tok, :] = y.astype(BF16)
        return 0

    lax.fori_loop(0, n_chunks, pass_b, 0, unroll=4)


def _mixer_specs(seq, n_chunks, row_blocks, key_block):
    zspec = lambda zb: pl.BlockSpec((None, seq, D_HEAD), lambda b, h: (zb + h, b, 0))
    kspec = pl.BlockSpec((None, n_chunks, D_HEAD, CHUNK), lambda b, h: (key_block + h, b, 0, 0))
    q_block, v_block, gate_block = row_blocks
    return [zspec(q_block), kspec, zspec(v_block), zspec(gate_block)]


def _mlstm(z3, zk, pack, arow, wsrow, decrow, norm_w, batch, seq):
    n_chunks = seq // CHUNK
    rows = pl.BlockSpec((n_chunks * SUBLANES, CHUNK), lambda b, h: (b, 0))
    return pl.pallas_call(
        functools.partial(_mlstm_kernel, n_chunks=n_chunks),
        out_shape=jax.ShapeDtypeStruct((N_HEADS, batch * seq, D_HEAD), BF16),
        grid=(batch, N_HEADS),
        in_specs=_mixer_specs(seq, n_chunks, (ZB_MQ, ZB_MV, ZB_MO), KB_M) + [
            pl.BlockSpec((None, None, seq, LANES), lambda b, h: (b, h, 0, 0)),
            rows, rows, rows,
            pl.BlockSpec((None, 1, D_HEAD), lambda b, h: (h, 0, 0)),
        ],
        out_specs=pl.BlockSpec((None, seq, D_HEAD), lambda b, h: (h, b, 0)),
        scratch_shapes=[
            pltpu.VMEM((2, D_HEAD, D_STATE), F32),
            pltpu.VMEM((2, n_chunks, D_HEAD, D_STATE), BF16),
        ],
        compiler_params=_params("arbitrary", "arbitrary"),
        name="mlstm",
    )(z3, zk, z3, z3, pack, arow, wsrow, decrow, norm_w)


def _ret_kernel(q_ref, kt_ref, v_ref, g_ref, dsum_ref, xi_ref, zeta_ref, cd_ref, y_ref, st_s, rst_s, *, n_chunks):
    st_s[...] = jnp.zeros_like(st_s)

    def state_step(direction, j):
        st = st_s[direction]
        rst_s[direction, j] = st.astype(BF16)
        kz = (kt_ref[j].astype(F32) * zeta_ref[direction]).astype(BF16)
        cd = jnp.concatenate([cd_ref[direction]] * (D_HEAD // LANES), axis=1)
        st_s[direction] = cd * st + jnp.dot(kz, v_ref[_chunk_tokens(j), :], preferred_element_type=F32)

    def pass_a(i, _):
        state_step(0, i)
        state_step(1, n_chunks - 1 - i)
        return 0

    lax.fori_loop(0, n_chunks, pass_a, 0, unroll=4)

    def pass_b(j, _):
        tok = _chunk_tokens(j)
        q = q_ref[tok, :]
        sc = (jnp.dot(q, kt_ref[j], preferred_element_type=F32) * dsum_ref[...]).astype(BF16)
        yr = jnp.dot(sc, v_ref[tok, :], preferred_element_type=F32)
        for direction in range(2):
            yr = yr + xi_ref[direction] * jnp.dot(q, rst_s[direction, j], preferred_element_type=F32)
        yr = yr * lax.rsqrt(jnp.mean(yr * yr, axis=-1, keepdims=True) + NORM_EPS)
        y_ref[tok, :] = (yr * g_ref[tok, :].astype(F32)).astype(BF16)
        return 0

    lax.fori_loop(0, n_chunks, pass_b, 0, unroll=8)


def _retention(z3, zk, dsum, xi, zeta, cd, batch, seq):
    n_chunks = seq // CHUNK
    per_head = lambda *tail: pl.BlockSpec((None,) + tail, lambda b, h: (h,) + (0,) * len(tail))
    return pl.pallas_call(
        functools.partial(_ret_kernel, n_chunks=n_chunks),
        out_shape=jax.ShapeDtypeStruct((N_HEADS, batch * seq, D_HEAD), BF16),
        grid=(batch, N_HEADS),
        in_specs=_mixer_specs(seq, n_chunks, (ZB_RQ, ZB_RV, ZB_RG), KB_R) + [
            per_head(CHUNK, CHUNK), per_head(2, CHUNK, D_HEAD), per_head(2, 1, CHUNK), per_head(2, 1, LANES),
        ],
        out_specs=pl.BlockSpec((None, seq, D_HEAD), lambda b, h: (h, b, 0)),
        scratch_shapes=[
            pltpu.VMEM((2, D_HEAD, D_HEAD), F32),
            pltpu.VMEM((2, n_chunks, D_HEAD, D_HEAD), BF16),
        ],
        compiler_params=_params("arbitrary", "arbitrary"),
        name="retention",
    )(z3, zk, z3, z3, dsum, xi, zeta, cd)


OUTPROJ_TM = 512


def _outproj_kernel(ym_ref, yr_ref, x_ref, g_ref, w_ref, lg_ref, lb_ref, o_ref, y_s):
    for grp, y_ref in enumerate((ym_ref, yr_ref)):
        for c in range(N_HEADS):
            c0 = grp * D_GROUP + c * D_HEAD
            y_s[:, c0:c0 + D_HEAD] = y_ref[c]
    acc = jnp.dot(y_s[...], w_ref[...], preferred_element_type=F32)
    v = DEEPNORM_ALPHA * x_ref[...] + g_ref[...] * acc
    o_ref[...] = _layernorm(v, lg_ref[...], lb_ref[...])


def _out_proj(ym, yr, x2, ada, mod_spec, w_out, ln_g, ln_b, seq):
    tokens = x2.shape[0]
    tm = OUTPROJ_TM
    tiles_per_seq = seq // tm
    row = pl.BlockSpec((1, D_MODEL), lambda i: (0, 0))
    return pl.pallas_call(
        _outproj_kernel,
        out_shape=jax.ShapeDtypeStruct((tokens, D_MODEL), F32),
        grid=(tokens // tm,),
        in_specs=[
            pl.BlockSpec((N_HEADS, tm, D_HEAD), lambda i: (0, i, 0)),
            pl.BlockSpec((N_HEADS, tm, D_HEAD), lambda i: (0, i, 0)),
            pl.BlockSpec((tm, D_MODEL), lambda i: (i, 0)),
            mod_spec(ADA_G1, tiles_per_seq),
            pl.BlockSpec((D_MODEL, D_MODEL), lambda i: (0, 0)),
            row, row,
        ],
        out_specs=pl.BlockSpec((tm, D_MODEL), lambda i: (i, 0)),
        scratch_shapes=[pltpu.VMEM((tm, D_MODEL), BF16)],
        compiler_params=_params("arbitrary"),
        name="out_proj",
    )(ym, yr, x2, ada, w_out, ln_g, ln_b)


FFN_TM = 512
FFN_TF = 512
FFN_BLOCKS_PER_STEP = 2
HALO = SUBLANES
N_BLOCK_REFS = 7


def _ffn_kernel(x_ref, xp_ref, xn_ref, sc_ref, sh_ref, g_ref, *refs, tiles_per_seq, n_steps, blocks_in_last):
    n_w = N_BLOCK_REFS * FFN_BLOCKS_PER_STEP
    block_refs = [refs[k * N_BLOCK_REFS:(k + 1) * N_BLOCK_REFS] for k in range(FFN_BLOCKS_PER_STEP)]
    lg_ref, lb_ref, o_ref, h_s, acc_s = refs[n_w:]
    i = pl.program_id(0)
    f = pl.program_id(1)
    tm = FFN_TM

    @pl.when(f == 0)
    def _():
        scale = 1.0 + sc_ref[...]
        shift = sh_ref[...]
        has_prev = (i % tiles_per_seq != 0).astype(F32)
        has_next = (i % tiles_per_seq != tiles_per_seq - 1).astype(F32)
        h_s[0:HALO, :] = ((xp_ref[...] * scale + shift) * has_prev).astype(BF16)
        h_s[HALO:HALO + tm, :] = (x_ref[...] * scale + shift).astype(BF16)
        h_s[HALO + tm:, :] = ((xn_ref[...] * scale + shift) * has_next).astype(BF16)
        acc_s[...] = jnp.zeros_like(acc_s)

    def run_blocks(n_blocks):
        h = h_s[...]
        update = None
        for wa_ref, wg_ref, cwa_ref, cwg_ref, cba_ref, cbg_ref, wd_ref in block_refs[:n_blocks]:
            def conv_branch(w_ref, cw_ref, cb_ref):
                u = jnp.dot(h, w_ref[...], preferred_element_type=F32)
                cw = cw_ref[...]
                return (cw[0:1, :] * u[HALO - 1:HALO - 1 + tm, :] + cw[1:2, :] * u[HALO:HALO + tm, :]
                        + cw[2:3, :] * u[HALO + 1:HALO + 1 + tm, :] + cb_ref[...])

            a = conv_branch(wa_ref, cwa_ref, cba_ref)
            g = conv_branch(wg_ref, cwg_ref, cbg_ref)
            act = (a * _sigmoid(a) * g).astype(BF16)
            contrib = jnp.dot(act, wd_ref[...], preferred_element_type=F32)
            update = contrib if update is None else update + contrib
        acc_s[...] += update

    if blocks_in_last == FFN_BLOCKS_PER_STEP:
        run_blocks(FFN_BLOCKS_PER_STEP)
    else:
        pl.when(f < n_steps - 1)(lambda: run_blocks(FFN_BLOCKS_PER_STEP))
        pl.when(f == n_steps - 1)(lambda: run_blocks(blocks_in_last))

    @pl.when(f == n_steps - 1)
    def _():
        v = DEEPNORM_ALPHA * x_ref[...] + g_ref[...] * acc_s[...]
        o_ref[...] = _layernorm(v, lg_ref[...], lb_ref[...])


def _ffn(x1, ada, mod_spec, w_up, conv_w, conv_b, w_down, ln_g, ln_b, seq):
    tokens = x1.shape[0]
    tm, tf = FFN_TM, FFN_TF
    tiles_per_seq = seq // tm
    n_f = D_FF // tf
    n_steps = pl.cdiv(n_f, FFN_BLOCKS_PER_STEP)
    blocks_in_last = n_f - (n_steps - 1) * FFN_BLOCKS_PER_STEP
    halo_per_tile = tm // HALO
    n_halo_blocks = tokens // HALO
    row = pl.BlockSpec((1, D_MODEL), lambda i, f: (0, 0))

    block_specs, block_args = [], []
    for k in range(FFN_BLOCKS_PER_STEP):
        blk = lambda f, k=k: jnp.minimum(f * FFN_BLOCKS_PER_STEP + k, n_f - 1)
        block_specs += [
            pl.BlockSpec((D_MODEL, tf), lambda i, f, blk=blk: (0, blk(f))),
            pl.BlockSpec((D_MODEL, tf), lambda i, f, blk=blk: (0, n_f + blk(f))),
            pl.BlockSpec((3, tf), lambda i, f, blk=blk: (0, blk(f))),
            pl.BlockSpec((3, tf), lambda i, f, blk=blk: (0, n_f + blk(f))),
            pl.BlockSpec((1, tf), lambda i, f, blk=blk: (0, blk(f))),
            pl.BlockSpec((1, tf), lambda i, f, blk=blk: (0, n_f + blk(f))),
            pl.BlockSpec((tf, D_MODEL), lambda i, f, blk=blk: (blk(f), 0)),
        ]
        block_args += [w_up, w_up, conv_w, conv_w, conv_b, conv_b, w_down]

    return pl.pallas_call(
        functools.partial(_ffn_kernel, tiles_per_seq=tiles_per_seq, n_steps=n_steps, blocks_in_last=blocks_in_last),
        out_shape=jax.ShapeDtypeStruct((tokens, D_MODEL), F32),
        grid=(tokens // tm, n_steps),
        in_specs=[
            pl.BlockSpec((tm, D_MODEL), lambda i, f: (i, 0)),
            pl.BlockSpec((HALO, D_MODEL), lambda i, f: (jnp.maximum(i * halo_per_tile - 1, 0), 0)),
            pl.BlockSpec((HALO, D_MODEL),
                         lambda i, f: (jnp.minimum((i + 1) * halo_per_tile, n_halo_blocks - 1), 0)),
            mod_spec(ADA_SC2, tiles_per_seq), mod_spec(ADA_SH2, tiles_per_seq), mod_spec(ADA_G2, tiles_per_seq),
        ] + block_specs + [row, row],
        out_specs=pl.BlockSpec((tm, D_MODEL), lambda i, f: (i, 0)),
        scratch_shapes=[
            pltpu.VMEM((tm + 2 * HALO, D_MODEL), BF16),
            pltpu.VMEM((tm, D_MODEL), F32),
        ],
        compiler_params=_params("arbitrary", "arbitrary"),
        name="ffn",
    )(x1, x1, x1, ada, ada, ada, *block_args, ln_g, ln_b)


def _rotary_tables(seq):
    half = D_HEAD // 2
    inv = 1.0 / (ROPE_BASE ** jnp.linspace(0.0, 1.0, half, dtype=F32))
    ang = jnp.arange(seq, dtype=F32)[:, None] * inv[None, :]
    return jnp.cos(ang), jnp.sin(ang)


def _retention_tables():
    hd = jnp.arange(N_HEADS, dtype=F32)
    lg = jnp.stack([jnp.log1p(-jnp.exp2(-RET_DECAY_EXP_FWD - hd)),
                    jnp.log1p(-jnp.exp2(-RET_DECAY_EXP_BWD - hd))], axis=1)
    pos = jnp.arange(CHUNK, dtype=F32)
    diff = pos[:, None] - pos[None, :]
    dmat = jnp.where(diff >= 0, jnp.exp(lg[:, :, None, None] * jnp.maximum(diff, 0.0)), 0.0)
    xi = jnp.exp(lg[:, :, None] * (pos + 1.0))
    zeta = jnp.exp(lg[:, :, None] * (CHUNK - 1.0 - pos))
    cd = jnp.exp(lg * CHUNK)
    dsum = dmat[:, 0] + dmat[:, 1, ::-1, ::-1]
    xi = jnp.stack([xi[:, 0], xi[:, 1, ::-1]], axis=1)
    zeta = jnp.stack([zeta[:, 0], zeta[:, 1, ::-1]], axis=1)
    xi = jnp.broadcast_to(xi[..., None], xi.shape + (D_HEAD,))
    zeta = zeta[:, :, None, :]
    cd = jnp.broadcast_to(cd[:, :, None, None], (N_HEADS, 2, 1, LANES))
    return dsum, xi, zeta, cd


def _split_pairs(w):
    w = w.reshape(D_MODEL, N_HEADS, D_HEAD // 2, 2)
    return jnp.concatenate([w[..., 0], w[..., 1]], axis=-1).reshape(D_MODEL, D_GROUP)


def _layout_w_in(w_in):
    sizes = [D_GROUP] * 4 + [2 * N_HEADS] * 2 + [D_GROUP] * 4
    mq, mk, mv, mo, mi, mf, rq, rk, rv, rg = jnp.split(w_in.astype(BF16), [int(s) for s in np.cumsum(sizes)[:-1]],
                                                       axis=1)
    w_main = jnp.concatenate([mq, mv, mo, _split_pairs(rq), rv, rg, mk, _split_pairs(rk)], axis=1)
    w_gates = jnp.pad(jnp.concatenate([mi, mf], axis=1), ((0, 0), (0, LANES - 4 * N_HEADS)))
    return w_main, w_gates


def _trunk(x, ada, batch_offset, weights, tables, rotary):
    batch, seq, _ = x.shape
    (w_main, w_gates, bi_col, bf_col, norm_w, w_out, ln1_g, ln1_b, w_up, conv_w, conv_b, w_down,
     ln2_g, ln2_b) = weights
    x2 = x.reshape(batch * seq, D_MODEL)
    cos, sin = rotary
    n_chunks = seq // CHUNK

    def mod_spec(which, tiles_per_seq):
        return pl.BlockSpec((None, None, 1, D_MODEL),
                            lambda i, *_: (which, batch_offset + i // tiles_per_seq, 0, 0))

    z3, zk, gi, gf = _in_proj(x2, ada, mod_spec, w_main, w_gates, cos, sin, seq)
    pack, arow, wsrow, decrow = _gates(gi, gf, jnp.tile(bi_col, (n_chunks, 1)), jnp.tile(bf_col, (n_chunks, 1)),
                                       batch, seq)
    ym = _mlstm(z3, zk, pack, arow, wsrow, decrow, norm_w, batch, seq)
    yr = _retention(z3, zk, *tables, batch, seq)
    x1 = _out_proj(ym, yr, x2, ada, mod_spec, w_out, ln1_g, ln1_b, seq)
    out = _ffn(x1, ada, mod_spec, w_up, conv_w, conv_b, w_down, ln2_g, ln2_b, seq)
    return out.reshape(batch, seq, D_MODEL)


def _prepare_weights(w_in, b_igate, b_fgate, mlstm_norm_w, w_out, ln1_g, ln1_b, w_up, conv_w, conv_b, w_down,
                     ln2_g, ln2_b):
    bcast_col = lambda b: jnp.broadcast_to(b[:, None], (2 * N_HEADS, CHUNK))
    return _layout_w_in(w_in[0]) + (
        bcast_col(b_igate[0]), bcast_col(b_fgate[0]),
        mlstm_norm_w[0].reshape(N_HEADS, 1, D_HEAD), w_out[0].astype(BF16), ln1_g, ln1_b,
        w_up[0].astype(BF16), conv_w[0], conv_b, w_down[0].astype(BF16), ln2_g, ln2_b,
    )


def kernel(x_prompt, x_sample, c_prompt, c_sample, w_ada, b_ada, w_in, b_igate, b_fgate, mlstm_norm_w, w_out,
           ln1_g, ln1_b, w_up, conv_w, conv_b, w_down, ln2_g, ln2_b):
    assert w_ada.shape[0] == DEPTH
    n_prompt = c_prompt.shape[0]
    c_all = jnp.concatenate([c_prompt, c_sample], axis=0)
    pad_rows = -c_all.shape[0] % SUBLANES
    c_all = jnp.pad(c_all, ((0, pad_rows), (0, 0)))
    ada = _ada(c_all, w_ada[0], b_ada[0][None, :])[:, :, None, :]

    weights = _prepare_weights(w_in, b_igate, b_fgate, mlstm_norm_w, w_out, ln1_g, ln1_b, w_up, conv_w, conv_b,
                               w_down, ln2_g, ln2_b)
    tables = _retention_tables()
    rotary = _rotary_tables(max(x_prompt.shape[1], x_sample.shape[1]))
    y_prompt = _trunk(x_prompt, ada, 0, weights, tables, rotary)
    y_sample = _trunk(x_sample, ada, n_prompt, weights, tables, rotary)
    return (y_prompt, y_sample)
```

```python
import functools

import jax
import jax.numpy as jnp
import numpy as np
from jax import lax
from jax.experimental import pallas as pl
from jax.experimental.pallas import tpu as pltpu

F32 = jnp.float32
BF16 = jnp.bfloat16

D_MODEL = 2048
N_HEADS = 4
D_HEAD = 256
D_GROUP = N_HEADS * D_HEAD
CHUNK = 128
D_FF = 5632
DEPTH = 1
DEEPNORM_ALPHA = (2.0 * DEPTH) ** 0.25
LN_EPS = 1e-5
NORM_EPS = 1e-6
ROPE_BASE = 10000.0
RET_DECAY_EXP_FWD = 5.0
RET_DECAY_EXP_BWD = 5.5
NEG_INIT = -1e30
HEAD_SCALE = D_HEAD ** -0.5

SUBLANES = 8
LANES = 128
VMEM_LIMIT_BYTES = 56 * 1024 * 1024

ADA_SH1, ADA_SC1, ADA_G1, ADA_SH2, ADA_SC2, ADA_G2 = range(6)

ZB_MQ, ZB_MV, ZB_MO, ZB_RQ, ZB_RV, ZB_RG = (N_HEADS * i for i in range(6))
N_ZBLOCKS = 6 * N_HEADS
KB_M, KB_R = 0, N_HEADS
N_KBLOCKS = 2 * N_HEADS

PK_M, PK_E, PK_IW = 0, 2, 4
N_PACK_ROWS = 6


def _sigmoid(x):
    return 1.0 / (1.0 + jnp.exp(-x))


def _log_sigmoid(x):
    return jnp.minimum(x, 0.0) - jnp.log1p(jnp.exp(-jnp.abs(x)))


def _layernorm(v, g, b):
    mu = jnp.mean(v, axis=-1, keepdims=True)
    d = v - mu
    var = jnp.mean(d * d, axis=-1, keepdims=True)
    return d * lax.rsqrt(var + LN_EPS) * g + b


def _params(*semantics):
    return pltpu.CompilerParams(dimension_semantics=semantics, vmem_limit_bytes=VMEM_LIMIT_BYTES)


ADA_TN = 1024


def _ada_kernel(c_ref, w_ref, b_ref, o_ref):
    c = c_ref[...]
    s = (c * _sigmoid(c)).astype(BF16)
    o_ref[...] = jnp.dot(s, w_ref[...].astype(BF16), preferred_element_type=F32) + b_ref[...]


def _ada(c, w_ada, b_ada):
    rows = c.shape[0]
    n_out = w_ada.shape[1]
    per_vec = D_MODEL // ADA_TN
    return pl.pallas_call(
        _ada_kernel,
        out_shape=jax.ShapeDtypeStruct((n_out // D_MODEL, rows, D_MODEL), F32),
        grid=(n_out // ADA_TN,),
        in_specs=[
            pl.BlockSpec((rows, D_MODEL), lambda n: (0, 0)),
            pl.BlockSpec((D_MODEL, ADA_TN), lambda n: (0, n)),
            pl.BlockSpec((1, ADA_TN), lambda n: (0, n)),
        ],
        out_specs=pl.BlockSpec((None, rows, ADA_TN), lambda n: (n // per_vec, 0, n % per_vec)),
        compiler_params=_params("arbitrary"),
        name="ada",
    )(c, w_ada, b_ada)


INPROJ_TM = 1024
INPROJ_TN = D_GROUP
CHUNKS_PER_TM = INPROJ_TM // CHUNK
N_ROW_STEPS = N_ZBLOCKS // N_HEADS
N_KEY_STEPS = N_KBLOCKS // N_HEADS


def _inproj_kernel(x_ref, sc_ref, sh_ref, w_ref, wg_ref, cos_ref, sin_ref, z_ref, zk_ref, gi_ref, gf_ref, h_ref):
    n = pl.program_id(1)
    half = D_HEAD // 2
    n_gates = 2 * N_HEADS

    @pl.when(n == 0)
    def _():
        hb = (x_ref[...] * (1.0 + sc_ref[...]) + sh_ref[...]).astype(BF16)
        h_ref[...] = hb
        g = jnp.dot(hb, wg_ref[...], preferred_element_type=F32)
        for c in range(CHUNKS_PER_TM):
            gt = g[c * CHUNK:(c + 1) * CHUNK, :].T
            gi_ref[c * SUBLANES:(c + 1) * SUBLANES, :] = gt[0:n_gates, :]
            gf_ref[c * SUBLANES:(c + 1) * SUBLANES, :] = gt[n_gates:2 * n_gates, :]

    def step(epilogue):
        h = h_ref[...]
        for c in range(N_HEADS):
            epilogue(c, jnp.dot(h, w_ref[:, c * D_HEAD:(c + 1) * D_HEAD], preferred_element_type=F32))

    def elementwise(fn):
        def epilogue(c, t):
            z_ref[c] = fn(t).astype(BF16)
        return epilogue

    def rotate(t):
        a = t[:, 0:half]
        b = t[:, half:D_HEAD]
        return a * cos_ref[...] - b * sin_ref[...], a * sin_ref[...] + b * cos_ref[...]

    def rotary(c, t):
        ra, rb = rotate(t)
        z_ref[c, :, 0:half] = ra.astype(BF16)
        z_ref[c, :, half:D_HEAD] = rb.astype(BF16)

    def keys(halves_fn):
        def epilogue(c, t):
            for lo, part in zip((0, half), halves_fn(t)):
                for cc in range(CHUNKS_PER_TM):
                    zk_ref[c, cc, lo:lo + half, :] = part[cc * CHUNK:(cc + 1) * CHUNK, :].T.astype(BF16)
        return epilogue

    nb = lambda zb: zb // N_HEADS
    kb = lambda k: N_ROW_STEPS + k // N_HEADS
    pl.when(n == nb(ZB_MQ))(lambda: step(elementwise(lambda t: t * HEAD_SCALE)))
    pl.when((n == nb(ZB_MV)) | (n == nb(ZB_RV)))(lambda: step(elementwise(lambda t: t)))
    pl.when(n == nb(ZB_MO))(lambda: step(elementwise(_sigmoid)))
    pl.when(n == nb(ZB_RG))(lambda: step(elementwise(lambda t: t * _sigmoid(t))))
    pl.when(n == nb(ZB_RQ))(lambda: step(rotary))
    pl.when(n == kb(KB_M))(lambda: step(keys(lambda t: (t[:, 0:half], t[:, half:D_HEAD]))))
    pl.when(n == kb(KB_R))(lambda: step(keys(lambda t: tuple(r * HEAD_SCALE for r in rotate(t)))))


def _in_proj(x2, ada, mod_spec, w_main, wg, cos, sin, seq):
    tokens = x2.shape[0]
    tm, tn = INPROJ_TM, INPROJ_TN
    tiles_per_seq = seq // tm
    row_step = lambda n: jnp.minimum(n, N_ROW_STEPS - 1)
    key_step = lambda n: jnp.maximum(n - N_ROW_STEPS, 0)
    half = D_HEAD // 2
    return pl.pallas_call(
        _inproj_kernel,
        out_shape=(
            jax.ShapeDtypeStruct((N_ZBLOCKS, tokens, D_HEAD), BF16),
            jax.ShapeDtypeStruct((N_KBLOCKS, tokens // CHUNK, D_HEAD, CHUNK), BF16),
            jax.ShapeDtypeStruct((tokens // CHUNK * SUBLANES, CHUNK), F32),
            jax.ShapeDtypeStruct((tokens // CHUNK * SUBLANES, CHUNK), F32),
        ),
        grid=(tokens // tm, N_ROW_STEPS + N_KEY_STEPS),
        in_specs=[
            pl.BlockSpec((tm, D_MODEL), lambda i, n: (i, 0)),
            mod_spec(ADA_SC1, tiles_per_seq), mod_spec(ADA_SH1, tiles_per_seq),
            pl.BlockSpec((D_MODEL, tn), lambda i, n: (0, n)),
            pl.BlockSpec((D_MODEL, LANES), lambda i, n: (0, 0)),
            pl.BlockSpec((tm, half), lambda i, n: (i % tiles_per_seq, 0)),
            pl.BlockSpec((tm, half), lambda i, n: (i % tiles_per_seq, 0)),
        ],
        out_specs=(
            pl.BlockSpec((N_HEADS, tm, D_HEAD), lambda i, n: (row_step(n), i, 0)),
            pl.BlockSpec((N_HEADS, CHUNKS_PER_TM, D_HEAD, CHUNK), lambda i, n: (key_step(n), i, 0, 0)),
            pl.BlockSpec((CHUNKS_PER_TM * SUBLANES, CHUNK), lambda i, n: (i, 0)),
            pl.BlockSpec((CHUNKS_PER_TM * SUBLANES, CHUNK), lambda i, n: (i, 0)),
        ),
        scratch_shapes=[pltpu.VMEM((tm, D_MODEL), BF16)],
        compiler_params=_params("arbitrary", "arbitrary"),
        name="in_proj",
    )(x2, ada, ada, w_main, wg, cos, sin)


def _lane_scan(x, op, suffix, lane):
    k = 1
    while k < CHUNK:
        if suffix:
            shifted = pltpu.roll(x, CHUNK - k, axis=1)
            valid = lane < CHUNK - k
        else:
            shifted = pltpu.roll(x, k, axis=1)
            valid = lane >= k
        x = jnp.where(valid, op(x, shifted), x)
        k *= 2
    return x


def _lane_allreduce(x, op):
    k = 1
    while k < CHUNK:
        x = op(x, pltpu.roll(x, k, axis=1))
        k *= 2
    return x


def _gates_kernel(gi_ref, gf_ref, bi_ref, bf_ref, pack_ref, arow_ref, wsrow_ref, decrow_ref,
                  m_s, e_s, iw_s, amax_s, glast_s, mstf_s, mstb_s, *, n_chunks):
    rows = n_chunks * SUBLANES
    lane = lax.broadcasted_iota(jnp.int32, (rows, CHUNK), 1)
    is_bwd = (lax.broadcasted_iota(jnp.int32, (rows, CHUNK), 0) % SUBLANES) >= N_HEADS

    ig = gi_ref[...] + bi_ref[...]
    lf = _log_sigmoid(gf_ref[...] + bf_ref[...])
    g = jnp.where(is_bwd, _lane_scan(lf, jnp.add, True, lane), _lane_scan(lf, jnp.add, False, lane))
    a = ig - g
    cm = jnp.where(is_bwd, _lane_scan(a, jnp.maximum, True, lane), _lane_scan(a, jnp.maximum, False, lane))
    amax = _lane_allreduce(a, jnp.maximum)
    arow_ref[...] = a
    amax_s[...] = amax
    glast_s[...] = _lane_allreduce(lf, jnp.add)

    def rec(i, carry):
        mf, mb = carry
        rf = pl.ds(pl.multiple_of(i * SUBLANES, SUBLANES), SUBLANES)
        rb = pl.ds(pl.multiple_of((n_chunks - 1 - i) * SUBLANES, SUBLANES), SUBLANES)
        mstf_s[rf, :] = mf
        mstb_s[rb, :] = mb
        mf = glast_s[rf, :] + jnp.maximum(mf, amax_s[rf, :])
        mb = glast_s[rb, :] + jnp.maximum(mb, amax_s[rb, :])
        return mf, mb

    init = jnp.full((SUBLANES, CHUNK), NEG_INIT, F32)
    lax.fori_loop(0, n_chunks, rec, (init, init), unroll=True)

    mst = jnp.where(is_bwd, mstb_s[...], mstf_s[...])
    m_row = jnp.maximum(cm, mst)
    m_last = jnp.maximum(amax, mst)
    m_s[...] = m_row
    e_s[...] = jnp.exp(-(g + m_row))
    iw_s[...] = jnp.exp(mst - m_row)
    wsrow_ref[...] = jnp.exp(a - m_last)
    decrow_ref[...] = jnp.exp(mst - m_last)

    top_rows = SUBLANES
    assert N_PACK_ROWS <= top_rows
    top_row = lax.broadcasted_iota(jnp.int32, (top_rows, CHUNK), 0)
    pad = jnp.zeros((CHUNK - top_rows, CHUNK), F32)

    def emit(j, _):
        r0 = pl.multiple_of(j * SUBLANES, SUBLANES)
        tok = pl.ds(pl.multiple_of(j * CHUNK, CHUNK), CHUNK)
        quantities = [s[pl.ds(r0, SUBLANES), :] for s in (m_s, e_s, iw_s)]
        for h in range(N_HEADS):
            top = jnp.zeros((top_rows, CHUNK), F32)
            for qi, qt in enumerate(quantities):
                for direction in range(2):
                    src = N_HEADS * direction + h
                    top = jnp.where(top_row == 2 * qi + direction, qt[src:src + 1, :], top)
            tile = jnp.concatenate([top, pad], axis=0)
            pack_ref[h, tok, :] = tile.T
        return 0

    lax.fori_loop(0, n_chunks, emit, 0, unroll=True)


def _gates(gi, gf, bi_col, bf_col, batch, seq):
    n_chunks = seq // CHUNK
    rows = n_chunks * SUBLANES
    row_layout = jax.ShapeDtypeStruct((batch * rows, CHUNK), F32)
    per_batch = pl.BlockSpec((rows, CHUNK), lambda b: (b, 0))
    shared = pl.BlockSpec((rows, CHUNK), lambda b: (0, 0))
    return pl.pallas_call(
        functools.partial(_gates_kernel, n_chunks=n_chunks),
        out_shape=(jax.ShapeDtypeStruct((batch, N_HEADS, seq, LANES), F32), row_layout, row_layout, row_layout),
        grid=(batch,),
        in_specs=[per_batch, per_batch, shared, shared],
        out_specs=(pl.BlockSpec((None, N_HEADS, seq, LANES), lambda b: (b, 0, 0, 0)),
                   per_batch, per_batch, per_batch),
        scratch_shapes=[pltpu.VMEM((rows, CHUNK), F32) for _ in range(7)],
        compiler_params=_params("arbitrary"),
        name="gates",
    )(gi, gf, bi_col, bf_col)


def _causal_mask(direction):
    row = lax.broadcasted_iota(jnp.int32, (CHUNK, CHUNK), 0)
    col = lax.broadcasted_iota(jnp.int32, (CHUNK, CHUNK), 1)
    return (col <= row) if direction == 0 else (col >= row)


def _chunk_tokens(j):
    return pl.ds(pl.multiple_of(j * CHUNK, CHUNK), CHUNK)


D_STATE = D_HEAD + LANES

def _mlstm_kernel(q_ref, kt_ref, v_ref, o_ref, pack_ref, arow_ref, wsrow_ref, decrow_ref, nw_ref, y_ref,
                  st_s, cst_s, sc_s, hm_s, *, n_chunks):
    head = pl.program_id(1)
    ones = jnp.ones((CHUNK, LANES), BF16)

    def v_ext(j):
        return jnp.concatenate([v_ref[_chunk_tokens(j), :], ones], axis=1)

    def gate_row(direction, j):
        return pl.ds(j * SUBLANES + head + N_HEADS * direction, 1)

    st_s[...] = jnp.zeros_like(st_s)

    def state_step(direction, j):
        st = st_s[direction]
        cst_s[direction, j] = st.astype(BF16)
        kw = (kt_ref[j].astype(F32) * wsrow_ref[gate_row(direction, j), :]).astype(BF16)
        dec = decrow_ref[gate_row(direction, j), :]
        dec = jnp.concatenate([dec] * (D_STATE // LANES), axis=1)
        st_s[direction] = dec * st + jnp.dot(kw, v_ext(j), preferred_element_type=F32)

    def pass_a(i, _):
        state_step(0, i)
        state_step(1, n_chunks - 1 - i)
        return 0

    lax.fori_loop(0, n_chunks, pass_a, 0, unroll=True)

    def scores(j, _):
        tok = _chunk_tokens(j)
        s = jnp.dot(q_ref[tok, :], kt_ref[j], preferred_element_type=F32)
        pk = pack_ref[tok, :]
        for direction in range(2):
            m_row = pk[:, PK_M + direction:PK_M + direction + 1]
            a = arow_ref[gate_row(direction, j), :]
            sc_s[direction, j] = (s * jnp.where(_causal_mask(direction), jnp.exp(a - m_row), 0.0)).astype(BF16)
        return 0

    lax.fori_loop(0, n_chunks, scores, 0, unroll=True)

    def outputs(j, _):
        tok = _chunk_tokens(j)
        q = q_ref[tok, :]
        vx = v_ext(j)
        pk = pack_ref[tok, :]
        hm = None
        for direction in range(2):
            col = lambda base: pk[:, base + direction:base + direction + 1]
            e, iw = col(PK_E), col(PK_IW)
            tot = (jnp.dot(sc_s[direction, j], vx, preferred_element_type=F32)
                   + iw * jnp.dot(q, cst_s[direction, j], preferred_element_type=F32))
            r = 1.0 / jnp.maximum(jnp.abs(tot[:, D_HEAD:]), e)
            h = tot[:, :D_HEAD] * jnp.concatenate([r] * (D_HEAD // LANES), axis=1)
            hm = h if hm is None else hm + h
        hm_s[tok, :] = hm
        return 0

    lax.fori_loop(0, n_chunks, outputs, 0, unroll=True)

    def norm(j, _):
        tok = _chunk_tokens(j)
        hm = hm_s[tok, :]
        mu = jnp.mean(hm, axis=-1, keepdims=True)
        d = hm - mu
        var = jnp.mean(d * d, axis=-1, keepdims=True)
        y = d * lax.rsqrt(var + NORM_EPS) * nw_ref[...] * o_ref[tok, :].astype(F32)
        y_ref[tok, :] = y.astype(BF16)
        return 0

    lax.fori_loop(0, n_chunks, norm, 0, unroll=True)


def _mixer_specs(seq, n_chunks, row_blocks, key_block):
    zspec = lambda zb: pl.BlockSpec((None, seq, D_HEAD), lambda b, h: (zb + h, b, 0))
    kspec = pl.BlockSpec((None, n_chunks, D_HEAD, CHUNK), lambda b, h: (key_block + h, b, 0, 0))
    q_block, v_block, gate_block = row_blocks
    return [zspec(q_block), kspec, zspec(v_block), zspec(gate_block)]


def _mlstm(z3, zk, pack, arow, wsrow, decrow, norm_w, batch, seq):
    n_chunks = seq // CHUNK
    rows = pl.BlockSpec((n_chunks * SUBLANES, CHUNK), lambda b, h: (b, 0))
    return pl.pallas_call(
        functools.partial(_mlstm_kernel, n_chunks=n_chunks),
        out_shape=jax.ShapeDtypeStruct((N_HEADS, batch * seq, D_HEAD), BF16),
        grid=(batch, N_HEADS),
        in_specs=_mixer_specs(seq, n_chunks, (ZB_MQ, ZB_MV, ZB_MO), KB_M) + [
            pl.BlockSpec((None, None, seq, LANES), lambda b, h: (b, h, 0, 0)),
            rows, rows, rows,
            pl.BlockSpec((None, 1, D_HEAD), lambda b, h: (h, 0, 0)),
        ],
        out_specs=pl.BlockSpec((None, seq, D_HEAD), lambda b, h: (h, b, 0)),
        scratch_shapes=[
            pltpu.VMEM((2, D_HEAD, D_STATE), F32),
            pltpu.VMEM((2, n_chunks, D_HEAD, D_STATE), BF16),
            pltpu.VMEM((2, n_chunks, CHUNK, CHUNK), BF16),
            pltpu.VMEM((seq, D_HEAD), F32),
        ],
        compiler_params=_params("arbitrary", "arbitrary"),
        name="mlstm",
    )(z3, zk, z3, z3, pack, arow, wsrow, decrow, norm_w)


def _ret_kernel(q_ref, kt_ref, v_ref, g_ref, dsum_ref, xi_ref, zeta_ref, cd_ref, y_ref, st_s, rst_s, sc_s, *,
                n_chunks):
    st_s[...] = jnp.zeros_like(st_s)

    def state_step(direction, j):
        st = st_s[direction]
        rst_s[direction, j] = st.astype(BF16)
        kz = (kt_ref[j].astype(F32) * zeta_ref[direction]).astype(BF16)
        cd = jnp.concatenate([cd_ref[direction]] * (D_HEAD // LANES), axis=1)
        st_s[direction] = cd * st + jnp.dot(kz, v_ref[_chunk_tokens(j), :], preferred_element_type=F32)

    def pass_a(i, _):
        state_step(0, i)
        state_step(1, n_chunks - 1 - i)
        return 0

    lax.fori_loop(0, n_chunks, pass_a, 0, unroll=True)

    def scores(j, _):
        q = q_ref[_chunk_tokens(j), :]
        sc_s[j] = (jnp.dot(q, kt_ref[j], preferred_element_type=F32) * dsum_ref[...]).astype(BF16)
        return 0

    lax.fori_loop(0, n_chunks, scores, 0, unroll=True)

    def pass_b(j, _):
        tok = _chunk_tokens(j)
        q = q_ref[tok, :]
        yr = jnp.dot(sc_s[j], v_ref[tok, :], preferred_element_type=F32)
        for direction in range(2):
            yr = yr + xi_ref[direction] * jnp.dot(q, rst_s[direction, j], preferred_element_type=F32)
        yr = yr * lax.rsqrt(jnp.mean(yr * yr, axis=-1, keepdims=True) + NORM_EPS)
        y_ref[tok, :] = (yr * g_ref[tok, :].astype(F32)).astype(BF16)
        return 0

    lax.fori_loop(0, n_chunks, pass_b, 0, unroll=True)


def _retention(z3, zk, dsum, xi, zeta, cd, batch, seq):
    n_chunks = seq // CHUNK
    per_head = lambda *tail: pl.BlockSpec((None,) + tail, lambda b, h: (h,) + (0,) * len(tail))
    return pl.pallas_call(
        functools.partial(_ret_kernel, n_chunks=n_chunks),
        out_shape=jax.ShapeDtypeStruct((N_HEADS, batch * seq, D_HEAD), BF16),
        grid=(batch, N_HEADS),
        in_specs=_mixer_specs(seq, n_chunks, (ZB_RQ, ZB_RV, ZB_RG), KB_R) + [
            per_head(CHUNK, CHUNK), per_head(2, CHUNK, D_HEAD), per_head(2, 1, CHUNK), per_head(2, 1, LANES),
        ],
        out_specs=pl.BlockSpec((None, seq, D_HEAD), lambda b, h: (h, b, 0)),
        scratch_shapes=[
            pltpu.VMEM((2, D_HEAD, D_HEAD), F32),
            pltpu.VMEM((2, n_chunks, D_HEAD, D_HEAD), BF16),
            pltpu.VMEM((n_chunks, CHUNK, CHUNK), BF16),
        ],
        compiler_params=_params("arbitrary", "arbitrary"),
        name="retention",
    )(z3, zk, z3, z3, dsum, xi, zeta, cd)


OUTPROJ_TM = 512
OUTPROJ_PARTS = 2


def _outproj_kernel(ym_ref, yr_ref, x_ref, g_ref, w_ref, lg_ref, lb_ref, o_ref, y_s):
    for grp, y_ref in enumerate((ym_ref, yr_ref)):
        for c in range(N_HEADS):
            c0 = grp * D_GROUP + c * D_HEAD
            y_s[:, c0:c0 + D_HEAD] = y_ref[c]
    rows = OUTPROJ_TM // OUTPROJ_PARTS
    for p in range(OUTPROJ_PARTS):
        r = slice(p * rows, (p + 1) * rows)
        acc = jnp.dot(y_s[r, :], w_ref[...], preferred_element_type=F32)
        v = DEEPNORM_ALPHA * x_ref[r, :] + g_ref[...] * acc
        o_ref[r, :] = _layernorm(v, lg_ref[...], lb_ref[...])


def _out_proj(ym, yr, x2, ada, mod_spec, w_out, ln_g, ln_b, seq):
    tokens = x2.shape[0]
    tm = OUTPROJ_TM
    tiles_per_seq = seq // tm
    row = pl.BlockSpec((1, D_MODEL), lambda i: (0, 0))
    return pl.pallas_call(
        _outproj_kernel,
        out_shape=jax.ShapeDtypeStruct((tokens, D_MODEL), F32),
        grid=(tokens // tm,),
        in_specs=[
            pl.BlockSpec((N_HEADS, tm, D_HEAD), lambda i: (0, i, 0)),
            pl.BlockSpec((N_HEADS, tm, D_HEAD), lambda i: (0, i, 0)),
            pl.BlockSpec((tm, D_MODEL), lambda i: (i, 0)),
            mod_spec(ADA_G1, tiles_per_seq),
            pl.BlockSpec((D_MODEL, D_MODEL), lambda i: (0, 0)),
            row, row,
        ],
        out_specs=pl.BlockSpec((tm, D_MODEL), lambda i: (i, 0)),
        scratch_shapes=[pltpu.VMEM((tm, D_MODEL), BF16)],
        compiler_params=_params("arbitrary"),
        name="out_proj",
    )(ym, yr, x2, ada, w_out, ln_g, ln_b)


FFN_TM = 512
FFN_TF = 512
FFN_BLOCKS_PER_STEP = 2
HALO = SUBLANES
N_BLOCK_REFS = 7


def _ffn_kernel(x_ref, xp_ref, xn_ref, sc_ref, sh_ref, g_ref, *refs, tiles_per_seq, n_steps, blocks_in_first):
    n_w = N_BLOCK_REFS * FFN_BLOCKS_PER_STEP
    block_refs = [refs[k * N_BLOCK_REFS:(k + 1) * N_BLOCK_REFS] for k in range(FFN_BLOCKS_PER_STEP)]
    lg_ref, lb_ref, o_ref, h_s, acc_s = refs[n_w:]
    i = pl.program_id(0)
    f = pl.program_id(1)
    tm = FFN_TM

    @pl.when(f == 0)
    def _():
        scale = 1.0 + sc_ref[...]
        shift = sh_ref[...]
        has_prev = (i % tiles_per_seq != 0).astype(F32)
        has_next = (i % tiles_per_seq != tiles_per_seq - 1).astype(F32)
        h_s[0:HALO, :] = ((xp_ref[...] * scale + shift) * has_prev).astype(BF16)
        h_s[HALO:HALO + tm, :] = (x_ref[...] * scale + shift).astype(BF16)
        h_s[HALO + tm:, :] = ((xn_ref[...] * scale + shift) * has_next).astype(BF16)
        acc_s[...] = jnp.zeros_like(acc_s)

    def run_blocks(n_blocks):
        h = h_s[...]
        update = None
        for wa_ref, wg_ref, cwa_ref, cwg_ref, cba_ref, cbg_ref, wd_ref in block_refs[:n_blocks]:
            def conv_branch(w_ref, cw_ref, cb_ref):
                u = jnp.dot(h, w_ref[...], preferred_element_type=F32)
                cw = cw_ref[...]
                return (cw[0:1, :] * u[HALO - 1:HALO - 1 + tm, :] + cw[1:2, :] * u[HALO:HALO + tm, :]
                        + cw[2:3, :] * u[HALO + 1:HALO + 1 + tm, :] + cb_ref[...])

            a = conv_branch(wa_ref, cwa_ref, cba_ref)
            g = conv_branch(wg_ref, cwg_ref, cbg_ref)
            act = (a * _sigmoid(a) * g).astype(BF16)
            contrib = jnp.dot(act, wd_ref[...], preferred_element_type=F32)
            update = contrib if update is None else update + contrib
        acc_s[...] += update

    if blocks_in_first == FFN_BLOCKS_PER_STEP:
        run_blocks(FFN_BLOCKS_PER_STEP)
    else:
        pl.when(f == 0)(lambda: run_blocks(blocks_in_first))
        pl.when(f > 0)(lambda: run_blocks(FFN_BLOCKS_PER_STEP))

    @pl.when(f == n_steps - 1)
    def _():
        v = DEEPNORM_ALPHA * x_ref[...] + g_ref[...] * acc_s[...]
        o_ref[...] = _layernorm(v, lg_ref[...], lb_ref[...])


def _ffn(x1, ada, mod_spec, w_up, conv_w, conv_b, w_down, ln_g, ln_b, seq):
    tokens = x1.shape[0]
    tm, tf = FFN_TM, FFN_TF
    tiles_per_seq = seq // tm
    n_f = D_FF // tf
    n_steps = pl.cdiv(n_f, FFN_BLOCKS_PER_STEP)
    blocks_in_first = n_f - (n_steps - 1) * FFN_BLOCKS_PER_STEP
    halo_per_tile = tm // HALO
    n_halo_blocks = tokens // HALO
    row = pl.BlockSpec((1, D_MODEL), lambda i, f: (0, 0))

    block_specs, block_args = [], []
    for k in range(FFN_BLOCKS_PER_STEP):
        blk = lambda f, k=k: jnp.where((f == 0) & (k < blocks_in_first), k,
                                       blocks_in_first + jnp.maximum(f - 1, 0) * FFN_BLOCKS_PER_STEP + k)
        block_specs += [
            pl.BlockSpec((D_MODEL, tf), lambda i, f, blk=blk: (0, blk(f))),
            pl.BlockSpec((D_MODEL, tf), lambda i, f, blk=blk: (0, n_f + blk(f))),
            pl.BlockSpec((3, tf), lambda i, f, blk=blk: (0, blk(f))),
            pl.BlockSpec((3, tf), lambda i, f, blk=blk: (0, n_f + blk(f))),
            pl.BlockSpec((1, tf), lambda i, f, blk=blk: (0, blk(f))),
            pl.BlockSpec((1, tf), lambda i, f, blk=blk: (0, n_f + blk(f))),
            pl.BlockSpec((tf, D_MODEL), lambda i, f, blk=blk: (blk(f), 0)),
        ]
        block_args += [w_up, w_up, conv_w, conv_w, conv_b, conv_b, w_down]

    return pl.pallas_call(
        functools.partial(_ffn_kernel, tiles_per_seq=tiles_per_seq, n_steps=n_steps, blocks_in_first=blocks_in_first),
        out_shape=jax.ShapeDtypeStruct((tokens, D_MODEL), F32),
        grid=(tokens // tm, n_steps),
        in_specs=[
            pl.BlockSpec((tm, D_MODEL), lambda i, f: (i, 0)),
            pl.BlockSpec((HALO, D_MODEL), lambda i, f: (jnp.maximum(i * halo_per_tile - 1, 0), 0)),
            pl.BlockSpec((HALO, D_MODEL),
                         lambda i, f: (jnp.minimum((i + 1) * halo_per_tile, n_halo_blocks - 1), 0)),
            mod_spec(ADA_SC2, tiles_per_seq), mod_spec(ADA_SH2, tiles_per_seq), mod_spec(ADA_G2, tiles_per_seq),
        ] + block_specs + [row, row],
        out_specs=pl.BlockSpec((tm, D_MODEL), lambda i, f: (i, 0)),
        scratch_shapes=[
            pltpu.VMEM((tm + 2 * HALO, D_MODEL), BF16),
            pltpu.VMEM((tm, D_MODEL), F32),
        ],
        compiler_params=_params("arbitrary", "arbitrary"),
        name="ffn",
    )(x1, x1, x1, ada, ada, ada, *block_args, ln_g, ln_b)


def _rotary_tables(seq):
    half = D_HEAD // 2
    inv = 1.0 / (ROPE_BASE ** jnp.linspace(0.0, 1.0, half, dtype=F32))
    ang = jnp.arange(seq, dtype=F32)[:, None] * inv[None, :]
    return jnp.cos(ang), jnp.sin(ang)


def _retention_tables():
    hd = jnp.arange(N_HEADS, dtype=F32)
    lg = jnp.stack([jnp.log1p(-jnp.exp2(-RET_DECAY_EXP_FWD - hd)),
                    jnp.log1p(-jnp.exp2(-RET_DECAY_EXP_BWD - hd))], axis=1)
    pos = jnp.arange(CHUNK, dtype=F32)
    diff = pos[:, None] - pos[None, :]
    dmat = jnp.where(diff >= 0, jnp.exp(lg[:, :, None, None] * jnp.maximum(diff, 0.0)), 0.0)
    xi = jnp.exp(lg[:, :, None] * (pos + 1.0))
    zeta = jnp.exp(lg[:, :, None] * (CHUNK - 1.0 - pos))
    cd = jnp.exp(lg * CHUNK)
    dsum = dmat[:, 0] + dmat[:, 1, ::-1, ::-1]
    xi = jnp.stack([xi[:, 0], xi[:, 1, ::-1]], axis=1)
    zeta = jnp.stack([zeta[:, 0], zeta[:, 1, ::-1]], axis=1)
    xi = jnp.broadcast_to(xi[..., None], xi.shape + (D_HEAD,))
    zeta = zeta[:, :, None, :]
    cd = jnp.broadcast_to(cd[:, :, None, None], (N_HEADS, 2, 1, LANES))
    return dsum, xi, zeta, cd


def _split_pairs(w):
    w = w.reshape(D_MODEL, N_HEADS, D_HEAD // 2, 2)
    return jnp.concatenate([w[..., 0], w[..., 1]], axis=-1).reshape(D_MODEL, D_GROUP)


def _layout_w_in(w_in):
    sizes = [D_GROUP] * 4 + [2 * N_HEADS] * 2 + [D_GROUP] * 4
    mq, mk, mv, mo, mi, mf, rq, rk, rv, rg = jnp.split(w_in.astype(BF16), [int(s) for s in np.cumsum(sizes)[:-1]],
                                                       axis=1)
    w_main = jnp.concatenate([mq, mv, mo, _split_pairs(rq), rv, rg, mk, _split_pairs(rk)], axis=1)
    w_gates = jnp.pad(jnp.concatenate([mi, mf], axis=1), ((0, 0), (0, LANES - 4 * N_HEADS)))
    return w_main, w_gates


def _trunk(x, ada, batch_offset, weights, tables, rotary):
    batch, seq, _ = x.shape
    (w_main, w_gates, bi_col, bf_col, norm_w, w_out, ln1_g, ln1_b, w_up, conv_w, conv_b, w_down,
     ln2_g, ln2_b) = weights
    x2 = x.reshape(batch * seq, D_MODEL)
    cos, sin = rotary
    n_chunks = seq // CHUNK

    def mod_spec(which, tiles_per_seq):
        return pl.BlockSpec((None, None, 1, D_MODEL),
                            lambda i, *_: (which, batch_offset + i // tiles_per_seq, 0, 0))

    z3, zk, gi, gf = _in_proj(x2, ada, mod_spec, w_main, w_gates, cos, sin, seq)
    pack, arow, wsrow, decrow = _gates(gi, gf, jnp.tile(bi_col, (n_chunks, 1)), jnp.tile(bf_col, (n_chunks, 1)),
                                       batch, seq)
    ym = _mlstm(z3, zk, pack, arow, wsrow, decrow, norm_w, batch, seq)
    yr = _retention(z3, zk, *tables, batch, seq)
    x1 = _out_proj(ym, yr, x2, ada, mod_spec, w_out, ln1_g, ln1_b, seq)
    out = _ffn(x1, ada, mod_spec, w_up, conv_w, conv_b, w_down, ln2_g, ln2_b, seq)
    return out.reshape(batch, seq, D_MODEL)


def _prepare_weights(w_in, b_igate, b_fgate, mlstm_norm_w, w_out, ln1_g, ln1_b, w_up, conv_w, conv_b, w_down,
                     ln2_g, ln2_b):
    bcast_col = lambda b: jnp.broadcast_to(b[:, None], (2 * N_HEADS, CHUNK))
    return _layout_w_in(w_in[0]) + (
        bcast_col(b_igate[0]), bcast_col(b_fgate[0]),
        mlstm_norm_w[0].reshape(N_HEADS, 1, D_HEAD), w_out[0].astype(BF16), ln1_g, ln1_b,
        w_up[0].astype(BF16), conv_w[0], conv_b, w_down[0].astype(BF16), ln2_g, ln2_b,
    )


def kernel(x_prompt, x_sample, c_prompt, c_sample, w_ada, b_ada, w_in, b_igate, b_fgate, mlstm_norm_w, w_out,
           ln1_g, ln1_b, w_up, conv_w, conv_b, w_down, ln2_g, ln2_b):
    assert w_ada.shape[0] == DEPTH
    n_prompt = c_prompt.shape[0]
    c_all = jnp.concatenate([c_prompt, c_sample], axis=0)
    pad_rows = -c_all.shape[0] % SUBLANES
    c_all = jnp.pad(c_all, ((0, pad_rows), (0, 0)))
    ada = _ada(c_all, w_ada[0], b_ada[0][None, :])[:, :, None, :]

    weights = _prepare_weights(w_in, b_igate, b_fgate, mlstm_norm_w, w_out, ln1_g, ln1_b, w_up, conv_w, conv_b,
                               w_down, ln2_g, ln2_b)
    tables = _retention_tables()
    rotary = _rotary_tables(max(x_prompt.shape[1], x_sample.shape[1]))
    y_prompt = _trunk(x_prompt, ada, 0, weights, tables, rotary)
    y_sample = _trunk(x_sample, ada, n_prompt, weights, tables, rotary)
    return (y_prompt, y_sample)
```

```python
import functools

import jax
import jax.numpy as jnp
import numpy as np
from jax import lax
from jax.experimental import pallas as pl
from jax.experimental.pallas import tpu as pltpu

F32 = jnp.float32
BF16 = jnp.bfloat16

D_MODEL = 2048
N_HEADS = 4
D_HEAD = 256
D_GROUP = N_HEADS * D_HEAD
CHUNK = 128
D_FF = 5632
DEPTH = 1
DEEPNORM_ALPHA = (2.0 * DEPTH) ** 0.25
LN_EPS = 1e-5
NORM_EPS = 1e-6
ROPE_BASE = 10000.0
RET_DECAY_EXP_FWD = 5.0
RET_DECAY_EXP_BWD = 5.5
NEG_INIT = -1e30
HEAD_SCALE = D_HEAD ** -0.5

SUBLANES = 8
LANES = 128
VMEM_LIMIT_BYTES = 56 * 1024 * 1024

ADA_SH1, ADA_SC1, ADA_G1, ADA_SH2, ADA_SC2, ADA_G2 = range(6)

ZA_MQ, ZA_MV, ZA_RQ, ZA_RV = (N_HEADS * i for i in range(4))
ZB_MO, ZB_RG = 0, N_HEADS
KB_M, KB_R = 0, N_HEADS
N_INPROJ_STEPS = 4

PK_M, PK_E, PK_IW = 0, 2, 4
N_PACK_ROWS = 6


def _sigmoid(x):
    return 1.0 / (1.0 + jnp.exp(-x))


def _log_sigmoid(x):
    return jnp.minimum(x, 0.0) - jnp.log1p(jnp.exp(-jnp.abs(x)))


def _layernorm(v, g, b):
    mu = jnp.mean(v, axis=-1, keepdims=True)
    d = v - mu
    var = jnp.mean(d * d, axis=-1, keepdims=True)
    return d * lax.rsqrt(var + LN_EPS) * g + b


def _params(*semantics):
    return pltpu.CompilerParams(dimension_semantics=semantics, vmem_limit_bytes=VMEM_LIMIT_BYTES)


ADA_TN = 1024


def _ada_kernel(c_ref, w_ref, b_ref, o_ref):
    c = c_ref[...]
    s = (c * _sigmoid(c)).astype(BF16)
    o_ref[...] = jnp.dot(s, w_ref[...].astype(BF16), preferred_element_type=F32) + b_ref[...]


def _ada(c, w_ada, b_ada):
    rows = c.shape[0]
    n_out = w_ada.shape[1]
    per_vec = D_MODEL // ADA_TN
    return pl.pallas_call(
        _ada_kernel,
        out_shape=jax.ShapeDtypeStruct((n_out // D_MODEL, rows, D_MODEL), F32),
        grid=(n_out // ADA_TN,),
        in_specs=[
            pl.BlockSpec((rows, D_MODEL), lambda n: (0, 0)),
            pl.BlockSpec((D_MODEL, ADA_TN), lambda n: (0, n)),
            pl.BlockSpec((1, ADA_TN), lambda n: (0, n)),
        ],
        out_specs=pl.BlockSpec((None, rows, ADA_TN), lambda n: (n // per_vec, 0, n % per_vec)),
        compiler_params=_params("arbitrary"),
        name="ada",
    )(c, w_ada, b_ada)


INPROJ_TM = 1024
INPROJ_TN = D_GROUP
CHUNKS_PER_TM = INPROJ_TM // CHUNK


def _inproj_kernel(x_ref, sc_ref, sh_ref, wa_ref, wb_ref, wg_ref, cos_ref, sin_ref,
                   za_ref, zb_ref, zk_ref, gi_ref, gf_ref, h_ref):
    n = pl.program_id(1)
    half = D_HEAD // 2
    n_gates = 2 * N_HEADS

    @pl.when(n == 0)
    def _():
        hb = (x_ref[...] * (1.0 + sc_ref[...]) + sh_ref[...]).astype(BF16)
        h_ref[...] = hb
        g = jnp.dot(hb, wg_ref[...], preferred_element_type=F32)
        for c in range(CHUNKS_PER_TM):
            gt = g[c * CHUNK:(c + 1) * CHUNK, :].T
            gi_ref[c * SUBLANES:(c + 1) * SUBLANES, :] = gt[0:n_gates, :]
            gf_ref[c * SUBLANES:(c + 1) * SUBLANES, :] = gt[n_gates:2 * n_gates, :]

    def project(w_ref, epilogue):
        h = h_ref[...]
        for c in range(N_HEADS):
            epilogue(c, jnp.dot(h, w_ref[:, c * D_HEAD:(c + 1) * D_HEAD], preferred_element_type=F32))

    def elementwise(z_ref, fn):
        def epilogue(c, t):
            z_ref[c] = fn(t).astype(BF16)
        return epilogue

    def rotate(t):
        a = t[:, 0:half]
        b = t[:, half:D_HEAD]
        return a * cos_ref[...] - b * sin_ref[...], a * sin_ref[...] + b * cos_ref[...]

    def rotary(c, t):
        ra, rb = rotate(t)
        za_ref[c, :, 0:half] = ra.astype(BF16)
        za_ref[c, :, half:D_HEAD] = rb.astype(BF16)

    def keys(halves_fn):
        def epilogue(c, t):
            for lo, part in zip((0, half), halves_fn(t)):
                for cc in range(CHUNKS_PER_TM):
                    zk_ref[c, cc, lo:lo + half, :] = part[cc * CHUNK:(cc + 1) * CHUNK, :].T.astype(BF16)
        return epilogue

    identity = lambda t: t
    steps = [
        (elementwise(za_ref, lambda t: t * HEAD_SCALE), keys(lambda t: (t[:, 0:half], t[:, half:D_HEAD]))),
        (elementwise(za_ref, identity), elementwise(zb_ref, _sigmoid)),
        (rotary, keys(lambda t: tuple(r * HEAD_SCALE for r in rotate(t)))),
        (elementwise(za_ref, identity), elementwise(zb_ref, lambda t: t * _sigmoid(t))),
    ]
    assert len(steps) == N_INPROJ_STEPS
    for k, (first, second) in enumerate(steps):
        @pl.when(n == k)
        def _():
            project(wa_ref, first)
            project(wb_ref, second)


def _in_proj(x2, ada, mod_spec, w_main, wg, cos, sin, seq):
    tokens = x2.shape[0]
    tm, tn = INPROJ_TM, INPROJ_TN
    tiles_per_seq = seq // tm
    half = D_HEAD // 2
    head_blocks = lambda n_groups: jax.ShapeDtypeStruct((n_groups * N_HEADS, tokens, D_HEAD), BF16)
    return pl.pallas_call(
        _inproj_kernel,
        out_shape=(
            head_blocks(4), head_blocks(2),
            jax.ShapeDtypeStruct((2 * N_HEADS, tokens // CHUNK, D_HEAD, CHUNK), BF16),
            jax.ShapeDtypeStruct((tokens // CHUNK * SUBLANES, CHUNK), F32),
            jax.ShapeDtypeStruct((tokens // CHUNK * SUBLANES, CHUNK), F32),
        ),
        grid=(tokens // tm, N_INPROJ_STEPS),
        in_specs=[
            pl.BlockSpec((tm, D_MODEL), lambda i, n: (i, 0)),
            mod_spec(ADA_SC1, tiles_per_seq), mod_spec(ADA_SH1, tiles_per_seq),
            pl.BlockSpec((D_MODEL, tn), lambda i, n: (0, n)),
            pl.BlockSpec((D_MODEL, tn), lambda i, n: (0, N_INPROJ_STEPS + n)),
            pl.BlockSpec((D_MODEL, LANES), lambda i, n: (0, 0)),
            pl.BlockSpec((tm, half), lambda i, n: (i % tiles_per_seq, 0)),
            pl.BlockSpec((tm, half), lambda i, n: (i % tiles_per_seq, 0)),
        ],
        out_specs=(
            pl.BlockSpec((N_HEADS, tm, D_HEAD), lambda i, n: (n, i, 0)),
            pl.BlockSpec((N_HEADS, tm, D_HEAD), lambda i, n: (n // 2, i, 0)),
            pl.BlockSpec((N_HEADS, CHUNKS_PER_TM, D_HEAD, CHUNK), lambda i, n: (n // 2, i, 0, 0)),
            pl.BlockSpec((CHUNKS_PER_TM * SUBLANES, CHUNK), lambda i, n: (i, 0)),
            pl.BlockSpec((CHUNKS_PER_TM * SUBLANES, CHUNK), lambda i, n: (i, 0)),
        ),
        scratch_shapes=[pltpu.VMEM((tm, D_MODEL), BF16)],
        compiler_params=_params("arbitrary", "arbitrary"),
        name="in_proj",
    )(x2, ada, ada, w_main, w_main, wg, cos, sin)


def _lane_scan(x, op, suffix, lane):
    k = 1
    while k < CHUNK:
        if suffix:
            shifted = pltpu.roll(x, CHUNK - k, axis=1)
            valid = lane < CHUNK - k
        else:
            shifted = pltpu.roll(x, k, axis=1)
            valid = lane >= k
        x = jnp.where(valid, op(x, shifted), x)
        k *= 2
    return x


def _lane_allreduce(x, op):
    k = 1
    while k < CHUNK:
        x = op(x, pltpu.roll(x, k, axis=1))
        k *= 2
    return x


def _gates_kernel(gi_ref, gf_ref, bi_ref, bf_ref, pack_ref, arow_ref, wsrow_ref, decrow_ref,
                  m_s, e_s, iw_s, amax_s, glast_s, mstf_s, mstb_s, *, n_chunks):
    rows = n_chunks * SUBLANES
    lane = lax.broadcasted_iota(jnp.int32, (rows, CHUNK), 1)
    is_bwd = (lax.broadcasted_iota(jnp.int32, (rows, CHUNK), 0) % SUBLANES) >= N_HEADS

    ig = gi_ref[...] + bi_ref[...]
    lf = _log_sigmoid(gf_ref[...] + bf_ref[...])
    g = jnp.where(is_bwd, _lane_scan(lf, jnp.add, True, lane), _lane_scan(lf, jnp.add, False, lane))
    a = ig - g
    cm = jnp.where(is_bwd, _lane_scan(a, jnp.maximum, True, lane), _lane_scan(a, jnp.maximum, False, lane))
    amax = _lane_allreduce(a, jnp.maximum)
    arow_ref[...] = a
    amax_s[...] = amax
    glast_s[...] = _lane_allreduce(lf, jnp.add)

    def rec(i, carry):
        mf, mb = carry
        rf = pl.ds(pl.multiple_of(i * SUBLANES, SUBLANES), SUBLANES)
        rb = pl.ds(pl.multiple_of((n_chunks - 1 - i) * SUBLANES, SUBLANES), SUBLANES)
        mstf_s[rf, :] = mf
        mstb_s[rb, :] = mb
        mf = glast_s[rf, :] + jnp.maximum(mf, amax_s[rf, :])
        mb = glast_s[rb, :] + jnp.maximum(mb, amax_s[rb, :])
        return mf, mb

    init = jnp.full((SUBLANES, CHUNK), NEG_INIT, F32)
    lax.fori_loop(0, n_chunks, rec, (init, init), unroll=True)

    mst = jnp.where(is_bwd, mstb_s[...], mstf_s[...])
    m_row = jnp.maximum(cm, mst)
    m_last = jnp.maximum(amax, mst)
    m_s[...] = m_row
    e_s[...] = jnp.exp(-(g + m_row))
    iw_s[...] = jnp.exp(mst - m_row)
    wsrow_ref[...] = jnp.exp(a - m_last)
    decrow_ref[...] = jnp.exp(mst - m_last)

    top_rows = SUBLANES
    assert N_PACK_ROWS <= top_rows
    top_row = lax.broadcasted_iota(jnp.int32, (top_rows, CHUNK), 0)
    pad = jnp.zeros((CHUNK - top_rows, CHUNK), F32)

    def emit(j, _):
        r0 = pl.multiple_of(j * SUBLANES, SUBLANES)
        tok = pl.ds(pl.multiple_of(j * CHUNK, CHUNK), CHUNK)
        quantities = [s[pl.ds(r0, SUBLANES), :] for s in (m_s, e_s, iw_s)]
        for h in range(N_HEADS):
            top = jnp.zeros((top_rows, CHUNK), F32)
            for qi, qt in enumerate(quantities):
                for direction in range(2):
                    src = N_HEADS * direction + h
                    top = jnp.where(top_row == 2 * qi + direction, qt[src:src + 1, :], top)
            tile = jnp.concatenate([top, pad], axis=0)
            pack_ref[h, tok, :] = tile.T
        return 0

    lax.fori_loop(0, n_chunks, emit, 0, unroll=True)


def _gates(gi, gf, bi_col, bf_col, batch, seq):
    n_chunks = seq // CHUNK
    rows = n_chunks * SUBLANES
    row_layout = jax.ShapeDtypeStruct((batch * rows, CHUNK), F32)
    per_batch = pl.BlockSpec((rows, CHUNK), lambda b: (b, 0))
    shared = pl.BlockSpec((rows, CHUNK), lambda b: (0, 0))
    return pl.pallas_call(
        functools.partial(_gates_kernel, n_chunks=n_chunks),
        out_shape=(jax.ShapeDtypeStruct((batch, N_HEADS, seq, LANES), F32), row_layout, row_layout, row_layout),
        grid=(batch,),
        in_specs=[per_batch, per_batch, shared, shared],
        out_specs=(pl.BlockSpec((None, N_HEADS, seq, LANES), lambda b: (b, 0, 0, 0)),
                   per_batch, per_batch, per_batch),
        scratch_shapes=[pltpu.VMEM((rows, CHUNK), F32) for _ in range(7)],
        compiler_params=_params("arbitrary"),
        name="gates",
    )(gi, gf, bi_col, bf_col)


def _causal_mask(direction):
    row = lax.broadcasted_iota(jnp.int32, (CHUNK, CHUNK), 0)
    col = lax.broadcasted_iota(jnp.int32, (CHUNK, CHUNK), 1)
    return (col <= row) if direction == 0 else (col >= row)


def _chunk_tokens(j):
    return pl.ds(pl.multiple_of(j * CHUNK, CHUNK), CHUNK)


D_STATE = D_HEAD + LANES

def _mlstm_kernel(q_ref, kt_ref, v_ref, o_ref, pack_ref, arow_ref, wsrow_ref, decrow_ref, nw_ref, y_ref,
                  st_s, cst_s, sc_s, hm_s, *, n_chunks):
    head = pl.program_id(1)
    ones = jnp.ones((CHUNK, LANES), BF16)

    def v_ext(j):
        return jnp.concatenate([v_ref[_chunk_tokens(j), :], ones], axis=1)

    def gate_row(direction, j):
        return pl.ds(j * SUBLANES + head + N_HEADS * direction, 1)

    st_s[...] = jnp.zeros_like(st_s)

    def state_step(direction, j):
        st = st_s[direction]
        cst_s[direction, j] = st.astype(BF16)
        kw = (kt_ref[j].astype(F32) * wsrow_ref[gate_row(direction, j), :]).astype(BF16)
        dec = decrow_ref[gate_row(direction, j), :]
        dec = jnp.concatenate([dec] * (D_STATE // LANES), axis=1)
        st_s[direction] = dec * st + jnp.dot(kw, v_ext(j), preferred_element_type=F32)

    def pass_a(i, _):
        state_step(0, i)
        state_step(1, n_chunks - 1 - i)
        return 0

    lax.fori_loop(0, n_chunks, pass_a, 0, unroll=True)

    def scores(j, _):
        tok = _chunk_tokens(j)
        s = jnp.dot(q_ref[tok, :], kt_ref[j], preferred_element_type=F32)
        pk = pack_ref[tok, :]
        for direction in range(2):
            m_row = pk[:, PK_M + direction:PK_M + direction + 1]
            a = arow_ref[gate_row(direction, j), :]
            sc_s[direction, j] = (s * jnp.where(_causal_mask(direction), jnp.exp(a - m_row), 0.0)).astype(BF16)
        return 0

    lax.fori_loop(0, n_chunks, scores, 0, unroll=True)

    def outputs(j, _):
        tok = _chunk_tokens(j)
        q = q_ref[tok, :]
        vx = v_ext(j)
        pk = pack_ref[tok, :]
        hm = None
        for direction in range(2):
            col = lambda base: pk[:, base + direction:base + direction + 1]
            e, iw = col(PK_E), col(PK_IW)
            tot = (jnp.dot(sc_s[direction, j], vx, preferred_element_type=F32)
                   + iw * jnp.dot(q, cst_s[direction, j], preferred_element_type=F32))
            r = 1.0 / jnp.maximum(jnp.abs(tot[:, D_HEAD:]), e)
            h = tot[:, :D_HEAD] * jnp.concatenate([r] * (D_HEAD // LANES), axis=1)
            hm = h if hm is None else hm + h
        hm_s[tok, :] = hm
        return 0

    lax.fori_loop(0, n_chunks, outputs, 0, unroll=True)

    def norm(j, _):
        tok = _chunk_tokens(j)
        hm = hm_s[tok, :]
        mu = jnp.mean(hm, axis=-1, keepdims=True)
        d = hm - mu
        var = jnp.mean(d * d, axis=-1, keepdims=True)
        y = d * lax.rsqrt(var + NORM_EPS) * nw_ref[...] * o_ref[tok, :].astype(F32)
        y_ref[tok, :] = y.astype(BF16)
        return 0

    lax.fori_loop(0, n_chunks, norm, 0, unroll=True)


def _mixer_specs(seq, n_chunks, row_blocks, key_block):
    zspec = lambda zb: pl.BlockSpec((None, seq, D_HEAD), lambda b, h: (zb + h, b, 0))
    kspec = pl.BlockSpec((None, n_chunks, D_HEAD, CHUNK), lambda b, h: (key_block + h, b, 0, 0))
    q_block, v_block, gate_block = row_blocks
    return [zspec(q_block), kspec, zspec(v_block), zspec(gate_block)]


def _mlstm(za, zb, zk, pack, arow, wsrow, decrow, norm_w, batch, seq):
    n_chunks = seq // CHUNK
    rows = pl.BlockSpec((n_chunks * SUBLANES, CHUNK), lambda b, h: (b, 0))
    return pl.pallas_call(
        functools.partial(_mlstm_kernel, n_chunks=n_chunks),
        out_shape=jax.ShapeDtypeStruct((N_HEADS, batch * seq, D_HEAD), BF16),
        grid=(batch, N_HEADS),
        in_specs=_mixer_specs(seq, n_chunks, (ZA_MQ, ZA_MV, ZB_MO), KB_M) + [
            pl.BlockSpec((None, None, seq, LANES), lambda b, h: (b, h, 0, 0)),
            rows, rows, rows,
            pl.BlockSpec((None, 1, D_HEAD), lambda b, h: (h, 0, 0)),
        ],
        out_specs=pl.BlockSpec((None, seq, D_HEAD), lambda b, h: (h, b, 0)),
        scratch_shapes=[
            pltpu.VMEM((2, D_HEAD, D_STATE), F32),
            pltpu.VMEM((2, n_chunks, D_HEAD, D_STATE), BF16),
            pltpu.VMEM((2, n_chunks, CHUNK, CHUNK), BF16),
            pltpu.VMEM((seq, D_HEAD), F32),
        ],
        compiler_params=_params("arbitrary", "arbitrary"),
        name="mlstm",
    )(za, zk, za, zb, pack, arow, wsrow, decrow, norm_w)


def _ret_kernel(q_ref, kt_ref, v_ref, g_ref, dsum_ref, xi_ref, zeta_ref, cd_ref, y_ref, st_s, rst_s, sc_s, *,
                n_chunks):
    st_s[...] = jnp.zeros_like(st_s)

    def state_step(direction, j):
        st = st_s[direction]
        rst_s[direction, j] = st.astype(BF16)
        kz = (kt_ref[j].astype(F32) * zeta_ref[direction]).astype(BF16)
        cd = jnp.concatenate([cd_ref[direction]] * (D_HEAD // LANES), axis=1)
        st_s[direction] = cd * st + jnp.dot(kz, v_ref[_chunk_tokens(j), :], preferred_element_type=F32)

    def pass_a(i, _):
        state_step(0, i)
        state_step(1, n_chunks - 1 - i)
        return 0

    lax.fori_loop(0, n_chunks, pass_a, 0, unroll=True)

    def scores(j, _):
        q = q_ref[_chunk_tokens(j), :]
        sc_s[j] = (jnp.dot(q, kt_ref[j], preferred_element_type=F32) * dsum_ref[...]).astype(BF16)
        return 0

    lax.fori_loop(0, n_chunks, scores, 0, unroll=True)

    def pass_b(j, _):
        tok = _chunk_tokens(j)
        q = q_ref[tok, :]
        yr = jnp.dot(sc_s[j], v_ref[tok, :], preferred_element_type=F32)
        for direction in range(2):
            yr = yr + xi_ref[direction] * jnp.dot(q, rst_s[direction, j], preferred_element_type=F32)
        yr = yr * lax.rsqrt(jnp.mean(yr * yr, axis=-1, keepdims=True) + NORM_EPS)
        y_ref[tok, :] = (yr * g_ref[tok, :].astype(F32)).astype(BF16)
        return 0

    lax.fori_loop(0, n_chunks, pass_b, 0, unroll=True)


def _retention(za, zb, zk, dsum, xi, zeta, cd, batch, seq):
    n_chunks = seq // CHUNK
    per_head = lambda *tail: pl.BlockSpec((None,) + tail, lambda b, h: (h,) + (0,) * len(tail))
    return pl.pallas_call(
        functools.partial(_ret_kernel, n_chunks=n_chunks),
        out_shape=jax.ShapeDtypeStruct((N_HEADS, batch * seq, D_HEAD), BF16),
        grid=(batch, N_HEADS),
        in_specs=_mixer_specs(seq, n_chunks, (ZA_RQ, ZA_RV, ZB_RG), KB_R) + [
            per_head(CHUNK, CHUNK), per_head(2, CHUNK, D_HEAD), per_head(2, 1, CHUNK), per_head(2, 1, LANES),
        ],
        out_specs=pl.BlockSpec((None, seq, D_HEAD), lambda b, h: (h, b, 0)),
        scratch_shapes=[
            pltpu.VMEM((2, D_HEAD, D_HEAD), F32),
            pltpu.VMEM((2, n_chunks, D_HEAD, D_HEAD), BF16),
            pltpu.VMEM((n_chunks, CHUNK, CHUNK), BF16),
        ],
        compiler_params=_params("arbitrary", "arbitrary"),
        name="retention",
    )(za, zk, za, zb, dsum, xi, zeta, cd)


OUTPROJ_TM = 512
OUTPROJ_PARTS = 2


def _outproj_kernel(ym_ref, yr_ref, x_ref, g_ref, w_ref, lg_ref, lb_ref, o_ref, y_s):
    for grp, y_ref in enumerate((ym_ref, yr_ref)):
        for c in range(N_HEADS):
            c0 = grp * D_GROUP + c * D_HEAD
            y_s[:, c0:c0 + D_HEAD] = y_ref[c]
    rows = OUTPROJ_TM // OUTPROJ_PARTS
    for p in range(OUTPROJ_PARTS):
        r = slice(p * rows, (p + 1) * rows)
        acc = jnp.dot(y_s[r, :], w_ref[...], preferred_element_type=F32)
        v = DEEPNORM_ALPHA * x_ref[r, :] + g_ref[...] * acc
        o_ref[r, :] = _layernorm(v, lg_ref[...], lb_ref[...])


def _out_proj(ym, yr, x2, ada, mod_spec, w_out, ln_g, ln_b, seq):
    tokens = x2.shape[0]
    tm = OUTPROJ_TM
    tiles_per_seq = seq // tm
    row = pl.BlockSpec((1, D_MODEL), lambda i: (0, 0))
    return pl.pallas_call(
        _outproj_kernel,
        out_shape=jax.ShapeDtypeStruct((tokens, D_MODEL), F32),
        grid=(tokens // tm,),
        in_specs=[
            pl.BlockSpec((N_HEADS, tm, D_HEAD), lambda i: (0, i, 0)),
            pl.BlockSpec((N_HEADS, tm, D_HEAD), lambda i: (0, i, 0)),
            pl.BlockSpec((tm, D_MODEL), lambda i: (i, 0)),
            mod_spec(ADA_G1, tiles_per_seq),
            pl.BlockSpec((D_MODEL, D_MODEL), lambda i: (0, 0)),
            row, row,
        ],
        out_specs=pl.BlockSpec((tm, D_MODEL), lambda i: (i, 0)),
        scratch_shapes=[pltpu.VMEM((tm, D_MODEL), BF16)],
        compiler_params=_params("arbitrary"),
        name="out_proj",
    )(ym, yr, x2, ada, w_out, ln_g, ln_b)


FFN_TM = 512
FFN_TF = 512
FFN_BLOCKS_PER_STEP = 2
HALO = SUBLANES
N_BLOCK_REFS = 7


def _ffn_kernel(x_ref, xp_ref, xn_ref, sc_ref, sh_ref, g_ref, *refs, tiles_per_seq, n_steps, blocks_in_first):
    n_w = N_BLOCK_REFS * FFN_BLOCKS_PER_STEP
    block_refs = [refs[k * N_BLOCK_REFS:(k + 1) * N_BLOCK_REFS] for k in range(FFN_BLOCKS_PER_STEP)]
    lg_ref, lb_ref, o_ref, h_s, acc_s = refs[n_w:]
    i = pl.program_id(0)
    f = pl.program_id(1)
    tm = FFN_TM

    @pl.when(f == 0)
    def _():
        scale = 1.0 + sc_ref[...]
        shift = sh_ref[...]
        has_prev = (i % tiles_per_seq != 0).astype(F32)
        has_next = (i % tiles_per_seq != tiles_per_seq - 1).astype(F32)
        h_s[0:HALO, :] = ((xp_ref[...] * scale + shift) * has_prev).astype(BF16)
        h_s[HALO:HALO + tm, :] = (x_ref[...] * scale + shift).astype(BF16)
        h_s[HALO + tm:, :] = ((xn_ref[...] * scale + shift) * has_next).astype(BF16)
        acc_s[...] = jnp.zeros_like(acc_s)

    def run_blocks(n_blocks):
        h = h_s[...]
        update = None
        for wa_ref, wg_ref, cwa_ref, cwg_ref, cba_ref, cbg_ref, wd_ref in block_refs[:n_blocks]:
            def conv_branch(w_ref, cw_ref, cb_ref):
                u = jnp.dot(h, w_ref[...], preferred_element_type=F32)
                cw = cw_ref[...]
                return (cw[0:1, :] * u[HALO - 1:HALO - 1 + tm, :] + cw[1:2, :] * u[HALO:HALO + tm, :]
                        + cw[2:3, :] * u[HALO + 1:HALO + 1 + tm, :] + cb_ref[...])

            a = conv_branch(wa_ref, cwa_ref, cba_ref)
            g = conv_branch(wg_ref, cwg_ref, cbg_ref)
            act = (a * _sigmoid(a) * g).astype(BF16)
            contrib = jnp.dot(act, wd_ref[...], preferred_element_type=F32)
            update = contrib if update is None else update + contrib
        acc_s[...] += update

    if blocks_in_first == FFN_BLOCKS_PER_STEP:
        run_blocks(FFN_BLOCKS_PER_STEP)
    else:
        pl.when(f == 0)(lambda: run_blocks(blocks_in_first))
        pl.when(f > 0)(lambda: run_blocks(FFN_BLOCKS_PER_STEP))

    @pl.when(f == n_steps - 1)
    def _():
        v = DEEPNORM_ALPHA * x_ref[...] + g_ref[...] * acc_s[...]
        o_ref[...] = _layernorm(v, lg_ref[...], lb_ref[...])


def _ffn(x1, ada, mod_spec, w_up, conv_w, conv_b, w_down, ln_g, ln_b, seq):
    tokens = x1.shape[0]
    tm, tf = FFN_TM, FFN_TF
    tiles_per_seq = seq // tm
    n_f = D_FF // tf
    n_steps = pl.cdiv(n_f, FFN_BLOCKS_PER_STEP)
    blocks_in_first = n_f - (n_steps - 1) * FFN_BLOCKS_PER_STEP
    halo_per_tile = tm // HALO
    n_halo_blocks = tokens // HALO
    row = pl.BlockSpec((1, D_MODEL), lambda i, f: (0, 0))

    block_specs, block_args = [], []
    for k in range(FFN_BLOCKS_PER_STEP):
        blk = lambda f, k=k: jnp.where((f == 0) & (k < blocks_in_first), k,
                                       blocks_in_first + jnp.maximum(f - 1, 0) * FFN_BLOCKS_PER_STEP + k)
        block_specs += [
            pl.BlockSpec((D_MODEL, tf), lambda i, f, blk=blk: (0, blk(f))),
            pl.BlockSpec((D_MODEL, tf), lambda i, f, blk=blk: (0, n_f + blk(f))),
            pl.BlockSpec((3, tf), lambda i, f, blk=blk: (0, blk(f))),
            pl.BlockSpec((3, tf), lambda i, f, blk=blk: (0, n_f + blk(f))),
            pl.BlockSpec((1, tf), lambda i, f, blk=blk: (0, blk(f))),
            pl.BlockSpec((1, tf), lambda i, f, blk=blk: (0, n_f + blk(f))),
            pl.BlockSpec((tf, D_MODEL), lambda i, f, blk=blk: (blk(f), 0)),
        ]
        block_args += [w_up, w_up, conv_w, conv_w, conv_b, conv_b, w_down]

    return pl.pallas_call(
        functools.partial(_ffn_kernel, tiles_per_seq=tiles_per_seq, n_steps=n_steps, blocks_in_first=blocks_in_first),
        out_shape=jax.ShapeDtypeStruct((tokens, D_MODEL), F32),
        grid=(tokens // tm, n_steps),
        in_specs=[
            pl.BlockSpec((tm, D_MODEL), lambda i, f: (i, 0)),
            pl.BlockSpec((HALO, D_MODEL), lambda i, f: (jnp.maximum(i * halo_per_tile - 1, 0), 0)),
            pl.BlockSpec((HALO, D_MODEL),
                         lambda i, f: (jnp.minimum((i + 1) * halo_per_tile, n_halo_blocks - 1), 0)),
            mod_spec(ADA_SC2, tiles_per_seq), mod_spec(ADA_SH2, tiles_per_seq), mod_spec(ADA_G2, tiles_per_seq),
        ] + block_specs + [row, row],
        out_specs=pl.BlockSpec((tm, D_MODEL), lambda i, f: (i, 0)),
        scratch_shapes=[
            pltpu.VMEM((tm + 2 * HALO, D_MODEL), BF16),
            pltpu.VMEM((tm, D_MODEL), F32),
        ],
        compiler_params=_params("arbitrary", "arbitrary"),
        name="ffn",
    )(x1, x1, x1, ada, ada, ada, *block_args, ln_g, ln_b)


def _rotary_tables(seq):
    half = D_HEAD // 2
    inv = 1.0 / (ROPE_BASE ** jnp.linspace(0.0, 1.0, half, dtype=F32))
    ang = jnp.arange(seq, dtype=F32)[:, None] * inv[None, :]
    return jnp.cos(ang), jnp.sin(ang)


def _retention_tables():
    hd = jnp.arange(N_HEADS, dtype=F32)
    lg = jnp.stack([jnp.log1p(-jnp.exp2(-RET_DECAY_EXP_FWD - hd)),
                    jnp.log1p(-jnp.exp2(-RET_DECAY_EXP_BWD - hd))], axis=1)
    pos = jnp.arange(CHUNK, dtype=F32)
    diff = pos[:, None] - pos[None, :]
    dmat = jnp.where(diff >= 0, jnp.exp(lg[:, :, None, None] * jnp.maximum(diff, 0.0)), 0.0)
    xi = jnp.exp(lg[:, :, None] * (pos + 1.0))
    zeta = jnp.exp(lg[:, :, None] * (CHUNK - 1.0 - pos))
    cd = jnp.exp(lg * CHUNK)
    dsum = dmat[:, 0] + dmat[:, 1, ::-1, ::-1]
    xi = jnp.stack([xi[:, 0], xi[:, 1, ::-1]], axis=1)
    zeta = jnp.stack([zeta[:, 0], zeta[:, 1, ::-1]], axis=1)
    xi = jnp.broadcast_to(xi[..., None], xi.shape + (D_HEAD,))
    zeta = zeta[:, :, None, :]
    cd = jnp.broadcast_to(cd[:, :, None, None], (N_HEADS, 2, 1, LANES))
    return dsum, xi, zeta, cd


def _split_pairs(w):
    w = w.reshape(D_MODEL, N_HEADS, D_HEAD // 2, 2)
    return jnp.concatenate([w[..., 0], w[..., 1]], axis=-1).reshape(D_MODEL, D_GROUP)


def _layout_w_in(w_in):
    sizes = [D_GROUP] * 4 + [2 * N_HEADS] * 2 + [D_GROUP] * 4
    mq, mk, mv, mo, mi, mf, rq, rk, rv, rg = jnp.split(w_in.astype(BF16), [int(s) for s in np.cumsum(sizes)[:-1]],
                                                       axis=1)
    w_main = jnp.concatenate([mq, mv, _split_pairs(rq), rv, mk, mo, _split_pairs(rk), rg], axis=1)
    w_gates = jnp.pad(jnp.concatenate([mi, mf], axis=1), ((0, 0), (0, LANES - 4 * N_HEADS)))
    return w_main, w_gates


def _trunk(x, ada, batch_offset, weights, tables, rotary):
    batch, seq, _ = x.shape
    (w_main, w_gates, bi_col, bf_col, norm_w, w_out, ln1_g, ln1_b, w_up, conv_w, conv_b, w_down,
     ln2_g, ln2_b) = weights
    x2 = x.reshape(batch * seq, D_MODEL)
    cos, sin = rotary
    n_chunks = seq // CHUNK

    def mod_spec(which, tiles_per_seq):
        return pl.BlockSpec((None, None, 1, D_MODEL),
                            lambda i, *_: (which, batch_offset + i // tiles_per_seq, 0, 0))

    za, zb, zk, gi, gf = _in_proj(x2, ada, mod_spec, w_main, w_gates, cos, sin, seq)
    pack, arow, wsrow, decrow = _gates(gi, gf, jnp.tile(bi_col, (n_chunks, 1)), jnp.tile(bf_col, (n_chunks, 1)),
                                       batch, seq)
    ym = _mlstm(za, zb, zk, pack, arow, wsrow, decrow, norm_w, batch, seq)
    yr = _retention(za, zb, zk, *tables, batch, seq)
    x1 = _out_proj(ym, yr, x2, ada, mod_spec, w_out, ln1_g, ln1_b, seq)
    out = _ffn(x1, ada, mod_spec, w_up, conv_w, conv_b, w_down, ln2_g, ln2_b, seq)
    return out.reshape(batch, seq, D_MODEL)


def _prepare_weights(w_in, b_igate, b_fgate, mlstm_norm_w, w_out, ln1_g, ln1_b, w_up, conv_w, conv_b, w_down,
                     ln2_g, ln2_b):
    bcast_col = lambda b: jnp.broadcast_to(b[:, None], (2 * N_HEADS, CHUNK))
    return _layout_w_in(w_in[0]) + (
        bcast_col(b_igate[0]), bcast_col(b_fgate[0]),
        mlstm_norm_w[0].reshape(N_HEADS, 1, D_HEAD), w_out[0].astype(BF16), ln1_g, ln1_b,
        w_up[0].astype(BF16), conv_w[0], conv_b, w_down[0].astype(BF16), ln2_g, ln2_b,
    )


def kernel(x_prompt, x_sample, c_prompt, c_sample, w_ada, b_ada, w_in, b_igate, b_fgate, mlstm_norm_w, w_out,
           ln1_g, ln1_b, w_up, conv_w, conv_b, w_down, ln2_g, ln2_b):
    assert w_ada.shape[0] == DEPTH
    n_prompt = c_prompt.shape[0]
    c_all = jnp.concatenate([c_prompt, c_sample], axis=0)
    pad_rows = -c_all.shape[0] % SUBLANES
    c_all = jnp.pad(c_all, ((0, pad_rows), (0, 0)))
    ada = _ada(c_all, w_ada[0], b_ada[0][None, :])[:, :, None, :]

    weights = _prepare_weights(w_in, b_igate, b_fgate, mlstm_norm_w, w_out, ln1_g, ln1_b, w_up, conv_w, conv_b,
                               w_down, ln2_g, ln2_b)
    tables = _retention_tables()
    rotary = _rotary_tables(max(x_prompt.shape[1], x_sample.shape[1]))
    y_prompt = _trunk(x_prompt, ada, 0, weights, tables, rotary)
    y_sample = _trunk(x_sample, ada, n_prompt, weights, tables, rotary)
    return (y_prompt, y_sample)
```

```python
import functools

import jax
import jax.numpy as jnp
import numpy as np
from jax import lax
from jax.experimental import pallas as pl
from jax.experimental.pallas import tpu as pltpu

F32 = jnp.float32
BF16 = jnp.bfloat16

D_MODEL = 2048
N_HEADS = 4
D_HEAD = 256
D_GROUP = N_HEADS * D_HEAD
CHUNK = 128
D_FF = 5632
DEPTH = 1
DEEPNORM_ALPHA = (2.0 * DEPTH) ** 0.25
LN_EPS = 1e-5
NORM_EPS = 1e-6
ROPE_BASE = 10000.0
RET_DECAY_EXP_FWD = 5.0
RET_DECAY_EXP_BWD = 5.5
NEG_INIT = -1e30
HEAD_SCALE = D_HEAD ** -0.5

SUBLANES = 8
LANES = 128
VMEM_LIMIT_BYTES = 56 * 1024 * 1024

ADA_SH1, ADA_SC1, ADA_G1, ADA_SH2, ADA_SC2, ADA_G2 = range(6)

ZA_MQ, ZA_MV, ZA_RQ, ZA_RV = (N_HEADS * i for i in range(4))
ZB_MO, ZB_RG = 0, N_HEADS
KB_M, KB_R = 0, N_HEADS
N_INPROJ_STEPS = 4

PK_M, PK_E, PK_IW = 0, 2, 4
N_PACK_ROWS = 6


def _sigmoid(x):
    return 1.0 / (1.0 + jnp.exp(-x))


def _log_sigmoid(x):
    return jnp.minimum(x, 0.0) - jnp.log1p(jnp.exp(-jnp.abs(x)))


def _layernorm(v, g, b):
    mu = jnp.mean(v, axis=-1, keepdims=True)
    d = v - mu
    var = jnp.mean(d * d, axis=-1, keepdims=True)
    return d * lax.rsqrt(var + LN_EPS) * g + b


def _params(*semantics):
    return pltpu.CompilerParams(dimension_semantics=semantics, vmem_limit_bytes=VMEM_LIMIT_BYTES)


ADA_TN = 1024


def _ada_kernel(c_ref, w_ref, b_ref, o_ref):
    c = c_ref[...]
    s = (c * _sigmoid(c)).astype(BF16)
    o_ref[...] = jnp.dot(s, w_ref[...].astype(BF16), preferred_element_type=F32) + b_ref[...]


def _ada(c, w_ada, b_ada):
    rows = c.shape[0]
    n_out = w_ada.shape[1]
    per_vec = D_MODEL // ADA_TN
    return pl.pallas_call(
        _ada_kernel,
        out_shape=jax.ShapeDtypeStruct((n_out // D_MODEL, rows, D_MODEL), F32),
        grid=(n_out // ADA_TN,),
        in_specs=[
            pl.BlockSpec((rows, D_MODEL), lambda n: (0, 0)),
            pl.BlockSpec((D_MODEL, ADA_TN), lambda n: (0, n)),
            pl.BlockSpec((1, ADA_TN), lambda n: (0, n)),
        ],
        out_specs=pl.BlockSpec((None, rows, ADA_TN), lambda n: (n // per_vec, 0, n % per_vec)),
        compiler_params=_params("arbitrary"),
        name="ada",
    )(c, w_ada, b_ada)


INPROJ_TM = 1024
INPROJ_TN = D_GROUP
CHUNKS_PER_TM = INPROJ_TM // CHUNK


def _inproj_kernel(x_ref, sc_ref, sh_ref, wa_ref, wb_ref, wg_ref, cos_ref, sin_ref,
                   za_ref, zb_ref, zk_ref, gi_ref, gf_ref, h_ref):
    n = pl.program_id(1)
    half = D_HEAD // 2
    n_gates = 2 * N_HEADS

    @pl.when(n == 0)
    def _():
        hb = (x_ref[...] * (1.0 + sc_ref[...]) + sh_ref[...]).astype(BF16)
        h_ref[...] = hb
        g = jnp.dot(hb, wg_ref[...], preferred_element_type=F32)
        for c in range(CHUNKS_PER_TM):
            gt = g[c * CHUNK:(c + 1) * CHUNK, :].T
            gi_ref[c * SUBLANES:(c + 1) * SUBLANES, :] = gt[0:n_gates, :]
            gf_ref[c * SUBLANES:(c + 1) * SUBLANES, :] = gt[n_gates:2 * n_gates, :]

    def elementwise(z_ref, fn):
        def epilogue(c, t):
            z_ref[c] = fn(t).astype(BF16)
        return epilogue

    def rotate(t):
        a = t[:, 0:half]
        b = t[:, half:D_HEAD]
        return a * cos_ref[...] - b * sin_ref[...], a * sin_ref[...] + b * cos_ref[...]

    def rotary(c, t):
        ra, rb = rotate(t)
        za_ref[c, :, 0:half] = ra.astype(BF16)
        za_ref[c, :, half:D_HEAD] = rb.astype(BF16)

    def keys(halves_fn):
        def epilogue(c, t):
            for lo, part in zip((0, half), halves_fn(t)):
                for cc in range(CHUNKS_PER_TM):
                    zk_ref[c, cc, lo:lo + half, :] = part[cc * CHUNK:(cc + 1) * CHUNK, :].T.astype(BF16)
        return epilogue

    identity = lambda t: t
    steps = [
        (elementwise(za_ref, lambda t: t * HEAD_SCALE), keys(lambda t: (t[:, 0:half], t[:, half:D_HEAD]))),
        (elementwise(za_ref, identity), elementwise(zb_ref, _sigmoid)),
        (rotary, keys(lambda t: tuple(r * HEAD_SCALE for r in rotate(t)))),
        (elementwise(za_ref, identity), elementwise(zb_ref, lambda t: t * _sigmoid(t))),
    ]
    assert len(steps) == N_INPROJ_STEPS
    for k, (first, second) in enumerate(steps):
        @pl.when(n == k)
        def _():
            h = h_ref[...]
            for c in range(N_HEADS):
                for w_ref, epilogue in ((wa_ref, first), (wb_ref, second)):
                    epilogue(c, jnp.dot(h, w_ref[:, c * D_HEAD:(c + 1) * D_HEAD], preferred_element_type=F32))


def _in_proj(x2, ada, mod_spec, w_main, wg, cos, sin, seq):
    tokens = x2.shape[0]
    tm, tn = INPROJ_TM, INPROJ_TN
    tiles_per_seq = seq // tm
    half = D_HEAD // 2
    head_blocks = lambda n_groups: jax.ShapeDtypeStruct((n_groups * N_HEADS, tokens, D_HEAD), BF16)
    return pl.pallas_call(
        _inproj_kernel,
        out_shape=(
            head_blocks(4), head_blocks(2),
            jax.ShapeDtypeStruct((2 * N_HEADS, tokens // CHUNK, D_HEAD, CHUNK), BF16),
            jax.ShapeDtypeStruct((tokens // CHUNK * SUBLANES, CHUNK), F32),
            jax.ShapeDtypeStruct((tokens // CHUNK * SUBLANES, CHUNK), F32),
        ),
        grid=(tokens // tm, N_INPROJ_STEPS),
        in_specs=[
            pl.BlockSpec((tm, D_MODEL), lambda i, n: (i, 0)),
            mod_spec(ADA_SC1, tiles_per_seq), mod_spec(ADA_SH1, tiles_per_seq),
            pl.BlockSpec((D_MODEL, tn), lambda i, n: (0, n)),
            pl.BlockSpec((D_MODEL, tn), lambda i, n: (0, N_INPROJ_STEPS + n)),
            pl.BlockSpec((D_MODEL, LANES), lambda i, n: (0, 0)),
            pl.BlockSpec((tm, half), lambda i, n: (i % tiles_per_seq, 0)),
            pl.BlockSpec((tm, half), lambda i, n: (i % tiles_per_seq, 0)),
        ],
        out_specs=(
            pl.BlockSpec((N_HEADS, tm, D_HEAD), lambda i, n: (n, i, 0)),
            pl.BlockSpec((N_HEADS, tm, D_HEAD), lambda i, n: (n // 2, i, 0)),
            pl.BlockSpec((N_HEADS, CHUNKS_PER_TM, D_HEAD, CHUNK), lambda i, n: (n // 2, i, 0, 0)),
            pl.BlockSpec((CHUNKS_PER_TM * SUBLANES, CHUNK), lambda i, n: (i, 0)),
            pl.BlockSpec((CHUNKS_PER_TM * SUBLANES, CHUNK), lambda i, n: (i, 0)),
        ),
        scratch_shapes=[pltpu.VMEM((tm, D_MODEL), BF16)],
        compiler_params=_params("arbitrary", "arbitrary"),
        name="in_proj",
    )(x2, ada, ada, w_main, w_main, wg, cos, sin)


def _lane_scan(x, op, suffix, lane):
    k = 1
    while k < CHUNK:
        if suffix:
            shifted = pltpu.roll(x, CHUNK - k, axis=1)
            valid = lane < CHUNK - k
        else:
            shifted = pltpu.roll(x, k, axis=1)
            valid = lane >= k
        x = jnp.where(valid, op(x, shifted), x)
        k *= 2
    return x


def _lane_allreduce(x, op):
    k = 1
    while k < CHUNK:
        x = op(x, pltpu.roll(x, k, axis=1))
        k *= 2
    return x


def _gates_kernel(gi_ref, gf_ref, bi_ref, bf_ref, pack_ref, arow_ref, wsrow_ref, decrow_ref,
                  m_s, e_s, iw_s, amax_s, glast_s, mstf_s, mstb_s, *, n_chunks):
    rows = n_chunks * SUBLANES
    lane = lax.broadcasted_iota(jnp.int32, (rows, CHUNK), 1)
    is_bwd = (lax.broadcasted_iota(jnp.int32, (rows, CHUNK), 0) % SUBLANES) >= N_HEADS

    ig = gi_ref[...] + bi_ref[...]
    lf = _log_sigmoid(gf_ref[...] + bf_ref[...])
    g = jnp.where(is_bwd, _lane_scan(lf, jnp.add, True, lane), _lane_scan(lf, jnp.add, False, lane))
    a = ig - g
    cm = jnp.where(is_bwd, _lane_scan(a, jnp.maximum, True, lane), _lane_scan(a, jnp.maximum, False, lane))
    amax = _lane_allreduce(a, jnp.maximum)
    arow_ref[...] = a
    amax_s[...] = amax
    glast_s[...] = _lane_allreduce(lf, jnp.add)

    def rec(i, carry):
        mf, mb = carry
        rf = pl.ds(pl.multiple_of(i * SUBLANES, SUBLANES), SUBLANES)
        rb = pl.ds(pl.multiple_of((n_chunks - 1 - i) * SUBLANES, SUBLANES), SUBLANES)
        mstf_s[rf, :] = mf
        mstb_s[rb, :] = mb
        mf = glast_s[rf, :] + jnp.maximum(mf, amax_s[rf, :])
        mb = glast_s[rb, :] + jnp.maximum(mb, amax_s[rb, :])
        return mf, mb

    init = jnp.full((SUBLANES, CHUNK), NEG_INIT, F32)
    lax.fori_loop(0, n_chunks, rec, (init, init), unroll=True)

    mst = jnp.where(is_bwd, mstb_s[...], mstf_s[...])
    m_row = jnp.maximum(cm, mst)
    m_last = jnp.maximum(amax, mst)
    m_s[...] = m_row
    e_s[...] = jnp.exp(-(g + m_row))
    iw_s[...] = jnp.exp(mst - m_row)
    wsrow_ref[...] = jnp.exp(a - m_last)
    decrow_ref[...] = jnp.exp(mst - m_last)

    top_rows = SUBLANES
    assert N_PACK_ROWS <= top_rows
    top_row = lax.broadcasted_iota(jnp.int32, (top_rows, CHUNK), 0)
    pad = jnp.zeros((CHUNK - top_rows, CHUNK), F32)

    def emit(j, _):
        r0 = pl.multiple_of(j * SUBLANES, SUBLANES)
        tok = pl.ds(pl.multiple_of(j * CHUNK, CHUNK), CHUNK)
        quantities = [s[pl.ds(r0, SUBLANES), :] for s in (m_s, e_s, iw_s)]
        for h in range(N_HEADS):
            top = jnp.zeros((top_rows, CHUNK), F32)
            for qi, qt in enumerate(quantities):
                for direction in range(2):
                    src = N_HEADS * direction + h
                    top = jnp.where(top_row == 2 * qi + direction, qt[src:src + 1, :], top)
            tile = jnp.concatenate([top, pad], axis=0)
            pack_ref[h, tok, :] = tile.T
        return 0

    lax.fori_loop(0, n_chunks, emit, 0, unroll=True)


def _gates(gi, gf, bi_col, bf_col, batch, seq):
    n_chunks = seq // CHUNK
    rows = n_chunks * SUBLANES
    row_layout = jax.ShapeDtypeStruct((batch * rows, CHUNK), F32)
    per_batch = pl.BlockSpec((rows, CHUNK), lambda b: (b, 0))
    shared = pl.BlockSpec((rows, CHUNK), lambda b: (0, 0))
    return pl.pallas_call(
        functools.partial(_gates_kernel, n_chunks=n_chunks),
        out_shape=(jax.ShapeDtypeStruct((batch, N_HEADS, seq, LANES), F32), row_layout, row_layout, row_layout),
        grid=(batch,),
        in_specs=[per_batch, per_batch, shared, shared],
        out_specs=(pl.BlockSpec((None, N_HEADS, seq, LANES), lambda b: (b, 0, 0, 0)),
                   per_batch, per_batch, per_batch),
        scratch_shapes=[pltpu.VMEM((rows, CHUNK), F32) for _ in range(7)],
        compiler_params=_params("arbitrary"),
        name="gates",
    )(gi, gf, bi_col, bf_col)


def _causal_mask(direction):
    row = lax.broadcasted_iota(jnp.int32, (CHUNK, CHUNK), 0)
    col = lax.broadcasted_iota(jnp.int32, (CHUNK, CHUNK), 1)
    return (col <= row) if direction == 0 else (col >= row)


def _chunk_tokens(j):
    return pl.ds(pl.multiple_of(j * CHUNK, CHUNK), CHUNK)


D_STATE = D_HEAD + LANES

def _mlstm_kernel(q_ref, kt_ref, v_ref, o_ref, pack_ref, arow_ref, wsrow_ref, decrow_ref, nw_ref, y_ref,
                  st_s, cst_s, sc_s, hm_s, *, n_chunks):
    head = pl.program_id(1)
    ones = jnp.ones((CHUNK, LANES), BF16)

    def v_ext(j):
        return jnp.concatenate([v_ref[_chunk_tokens(j), :], ones], axis=1)

    def gate_row(direction, j):
        return pl.ds(j * SUBLANES + head + N_HEADS * direction, 1)

    st_s[...] = jnp.zeros_like(st_s)

    def state_step(direction, j):
        st = st_s[direction]
        cst_s[direction, j] = st.astype(BF16)
        kw = (kt_ref[j].astype(F32) * wsrow_ref[gate_row(direction, j), :]).astype(BF16)
        dec = decrow_ref[gate_row(direction, j), :]
        dec = jnp.concatenate([dec] * (D_STATE // LANES), axis=1)
        st_s[direction] = dec * st + jnp.dot(kw, v_ext(j), preferred_element_type=F32)

    def pass_a(i, _):
        state_step(0, i)
        state_step(1, n_chunks - 1 - i)
        return 0

    lax.fori_loop(0, n_chunks, pass_a, 0, unroll=True)

    def scores(j, _):
        tok = _chunk_tokens(j)
        s = jnp.dot(q_ref[tok, :], kt_ref[j], preferred_element_type=F32)
        pk = pack_ref[tok, :]
        for direction in range(2):
            m_row = pk[:, PK_M + direction:PK_M + direction + 1]
            a = arow_ref[gate_row(direction, j), :]
            sc_s[direction, j] = (s * jnp.where(_causal_mask(direction), jnp.exp(a - m_row), 0.0)).astype(BF16)
        return 0

    lax.fori_loop(0, n_chunks, scores, 0, unroll=True)

    def outputs(j, _):
        tok = _chunk_tokens(j)
        q = q_ref[tok, :]
        vx = v_ext(j)
        pk = pack_ref[tok, :]
        hm = None
        for direction in range(2):
            col = lambda base: pk[:, base + direction:base + direction + 1]
            e, iw = col(PK_E), col(PK_IW)
            tot = (jnp.dot(sc_s[direction, j], vx, preferred_element_type=F32)
                   + iw * jnp.dot(q, cst_s[direction, j], preferred_element_type=F32))
            r = 1.0 / jnp.maximum(jnp.abs(tot[:, D_HEAD:]), e)
            h = tot[:, :D_HEAD] * jnp.concatenate([r] * (D_HEAD // LANES), axis=1)
            hm = h if hm is None else hm + h
        hm_s[tok, :] = hm
        return 0

    lax.fori_loop(0, n_chunks, outputs, 0, unroll=True)

    def norm(j, _):
        tok = _chunk_tokens(j)
        hm = hm_s[tok, :]
        mu = jnp.mean(hm, axis=-1, keepdims=True)
        d = hm - mu
        var = jnp.mean(d * d, axis=-1, keepdims=True)
        y = d * lax.rsqrt(var + NORM_EPS) * nw_ref[...] * o_ref[tok, :].astype(F32)
        y_ref[tok, :] = y.astype(BF16)
        return 0

    lax.fori_loop(0, n_chunks, norm, 0, unroll=True)


def _mixer_specs(seq, n_chunks, row_blocks, key_block):
    zspec = lambda zb: pl.BlockSpec((None, seq, D_HEAD), lambda b, h: (zb + h, b, 0))
    kspec = pl.BlockSpec((None, n_chunks, D_HEAD, CHUNK), lambda b, h: (key_block + h, b, 0, 0))
    q_block, v_block, gate_block = row_blocks
    return [zspec(q_block), kspec, zspec(v_block), zspec(gate_block)]


def _mlstm(za, zb, zk, pack, arow, wsrow, decrow, norm_w, batch, seq):
    n_chunks = seq // CHUNK
    rows = pl.BlockSpec((n_chunks * SUBLANES, CHUNK), lambda b, h: (b, 0))
    return pl.pallas_call(
        functools.partial(_mlstm_kernel, n_chunks=n_chunks),
        out_shape=jax.ShapeDtypeStruct((N_HEADS, batch * seq, D_HEAD), BF16),
        grid=(batch, N_HEADS),
        in_specs=_mixer_specs(seq, n_chunks, (ZA_MQ, ZA_MV, ZB_MO), KB_M) + [
            pl.BlockSpec((None, None, seq, LANES), lambda b, h: (b, h, 0, 0)),
            rows, rows, rows,
            pl.BlockSpec((None, 1, D_HEAD), lambda b, h: (h, 0, 0)),
        ],
        out_specs=pl.BlockSpec((None, seq, D_HEAD), lambda b, h: (h, b, 0)),
        scratch_shapes=[
            pltpu.VMEM((2, D_HEAD, D_STATE), F32),
            pltpu.VMEM((2, n_chunks, D_HEAD, D_STATE), BF16),
            pltpu.VMEM((2, n_chunks, CHUNK, CHUNK), BF16),
            pltpu.VMEM((seq, D_HEAD), F32),
        ],
        compiler_params=_params("arbitrary", "arbitrary"),
        name="mlstm",
    )(za, zk, za, zb, pack, arow, wsrow, decrow, norm_w)


def _ret_kernel(q_ref, kt_ref, v_ref, g_ref, dsum_ref, xi_ref, zeta_ref, cd_ref, y_ref, st_s, rst_s, sc_s, *,
                n_chunks):
    st_s[...] = jnp.zeros_like(st_s)

    def state_step(direction, j):
        st = st_s[direction]
        rst_s[direction, j] = st.astype(BF16)
        kz = (kt_ref[j].astype(F32) * zeta_ref[direction]).astype(BF16)
        cd = jnp.concatenate([cd_ref[direction]] * (D_HEAD // LANES), axis=1)
        st_s[direction] = cd * st + jnp.dot(kz, v_ref[_chunk_tokens(j), :], preferred_element_type=F32)

    def pass_a(i, _):
        state_step(0, i)
        state_step(1, n_chunks - 1 - i)
        return 0

    lax.fori_loop(0, n_chunks, pass_a, 0, unroll=True)

    def scores(j, _):
        q = q_ref[_chunk_tokens(j), :]
        sc_s[j] = (jnp.dot(q, kt_ref[j], preferred_element_type=F32) * dsum_ref[...]).astype(BF16)
        return 0

    lax.fori_loop(0, n_chunks, scores, 0, unroll=True)

    def pass_b(j, _):
        tok = _chunk_tokens(j)
        q = q_ref[tok, :]
        yr = jnp.dot(sc_s[j], v_ref[tok, :], preferred_element_type=F32)
        for direction in range(2):
            yr = yr + xi_ref[direction] * jnp.dot(q, rst_s[direction, j], preferred_element_type=F32)
        yr = yr * lax.rsqrt(jnp.mean(yr * yr, axis=-1, keepdims=True) + NORM_EPS)
        y_ref[tok, :] = (yr * g_ref[tok, :].astype(F32)).astype(BF16)
        return 0

    lax.fori_loop(0, n_chunks, pass_b, 0, unroll=True)


def _retention(za, zb, zk, dsum, xi, zeta, cd, batch, seq):
    n_chunks = seq // CHUNK
    per_head = lambda *tail: pl.BlockSpec((None,) + tail, lambda b, h: (h,) + (0,) * len(tail))
    return pl.pallas_call(
        functools.partial(_ret_kernel, n_chunks=n_chunks),
        out_shape=jax.ShapeDtypeStruct((N_HEADS, batch * seq, D_HEAD), BF16),
        grid=(batch, N_HEADS),
        in_specs=_mixer_specs(seq, n_chunks, (ZA_RQ, ZA_RV, ZB_RG), KB_R) + [
            per_head(CHUNK, CHUNK), per_head(2, CHUNK, D_HEAD), per_head(2, 1, CHUNK), per_head(2, 1, LANES),
        ],
        out_specs=pl.BlockSpec((None, seq, D_HEAD), lambda b, h: (h, b, 0)),
        scratch_shapes=[
            pltpu.VMEM((2, D_HEAD, D_HEAD), F32),
            pltpu.VMEM((2, n_chunks, D_HEAD, D_HEAD), BF16),
            pltpu.VMEM((n_chunks, CHUNK, CHUNK), BF16),
        ],
        compiler_params=_params("arbitrary", "arbitrary"),
        name="retention",
    )(za, zk, za, zb, dsum, xi, zeta, cd)


OUTPROJ_TM = 512
OUTPROJ_PARTS = 2


def _outproj_kernel(ym_ref, yr_ref, x_ref, g_ref, w_ref, lg_ref, lb_ref, o_ref, y_s):
    for grp, y_ref in enumerate((ym_ref, yr_ref)):
        for c in range(N_HEADS):
            c0 = grp * D_GROUP + c * D_HEAD
            y_s[:, c0:c0 + D_HEAD] = y_ref[c]
    rows = OUTPROJ_TM // OUTPROJ_PARTS
    for p in range(OUTPROJ_PARTS):
        r = slice(p * rows, (p + 1) * rows)
        acc = jnp.dot(y_s[r, :], w_ref[...], preferred_element_type=F32)
        v = DEEPNORM_ALPHA * x_ref[r, :] + g_ref[...] * acc
        o_ref[r, :] = _layernorm(v, lg_ref[...], lb_ref[...])


def _out_proj(ym, yr, x2, ada, mod_spec, w_out, ln_g, ln_b, seq):
    tokens = x2.shape[0]
    tm = OUTPROJ_TM
    tiles_per_seq = seq // tm
    row = pl.BlockSpec((1, D_MODEL), lambda i: (0, 0))
    return pl.pallas_call(
        _outproj_kernel,
        out_shape=jax.ShapeDtypeStruct((tokens, D_MODEL), F32),
        grid=(tokens // tm,),
        in_specs=[
            pl.BlockSpec((N_HEADS, tm, D_HEAD), lambda i: (0, i, 0)),
            pl.BlockSpec((N_HEADS, tm, D_HEAD), lambda i: (0, i, 0)),
            pl.BlockSpec((tm, D_MODEL), lambda i: (i, 0)),
            mod_spec(ADA_G1, tiles_per_seq),
            pl.BlockSpec((D_MODEL, D_MODEL), lambda i: (0, 0)),
            row, row,
        ],
        out_specs=pl.BlockSpec((tm, D_MODEL), lambda i: (i, 0)),
        scratch_shapes=[pltpu.VMEM((tm, D_MODEL), BF16)],
        compiler_params=_params("arbitrary"),
        name="out_proj",
    )(ym, yr, x2, ada, w_out, ln_g, ln_b)


FFN_TM = 512
FFN_TF = 512
FFN_BLOCKS_PER_STEP = 2
HALO = SUBLANES
N_BLOCK_REFS = 7


def _ffn_kernel(x_ref, xp_ref, xn_ref, sc_ref, sh_ref, g_ref, *refs, tiles_per_seq, n_steps, blocks_in_first):
    n_w = N_BLOCK_REFS * FFN_BLOCKS_PER_STEP
    block_refs = [refs[k * N_BLOCK_REFS:(k + 1) * N_BLOCK_REFS] for k in range(FFN_BLOCKS_PER_STEP)]
    lg_ref, lb_ref, o_ref, h_s, acc_s = refs[n_w:]
    i = pl.program_id(0)
    f = pl.program_id(1)
    tm = FFN_TM

    @pl.when(f == 0)
    def _():
        scale = 1.0 + sc_ref[...]
        shift = sh_ref[...]
        has_prev = (i % tiles_per_seq != 0).astype(F32)
        has_next = (i % tiles_per_seq != tiles_per_seq - 1).astype(F32)
        h_s[0:HALO, :] = ((xp_ref[...] * scale + shift) * has_prev).astype(BF16)
        h_s[HALO:HALO + tm, :] = (x_ref[...] * scale + shift).astype(BF16)
        h_s[HALO + tm:, :] = ((xn_ref[...] * scale + shift) * has_next).astype(BF16)
        acc_s[...] = jnp.zeros_like(acc_s)

    def run_blocks(n_blocks):
        h = h_s[...]
        update = None
        for wa_ref, wg_ref, cwa_ref, cwg_ref, cba_ref, cbg_ref, wd_ref in block_refs[:n_blocks]:
            def conv_branch(w_ref, cw_ref, cb_ref):
                u = jnp.dot(h, w_ref[...], preferred_element_type=F32)
                cw = cw_ref[...]
                return (cw[0:1, :] * u[HALO - 1:HALO - 1 + tm, :] + cw[1:2, :] * u[HALO:HALO + tm, :]
                        + cw[2:3, :] * u[HALO + 1:HALO + 1 + tm, :] + cb_ref[...])

            a = conv_branch(wa_ref, cwa_ref, cba_ref)
            g = conv_branch(wg_ref, cwg_ref, cbg_ref)
            act = (a * _sigmoid(a) * g).astype(BF16)
            contrib = jnp.dot(act, wd_ref[...], preferred_element_type=F32)
            update = contrib if update is None else update + contrib
        acc_s[...] += update

    if blocks_in_first == FFN_BLOCKS_PER_STEP:
        run_blocks(FFN_BLOCKS_PER_STEP)
    else:
        pl.when(f == 0)(lambda: run_blocks(blocks_in_first))
        pl.when(f > 0)(lambda: run_blocks(FFN_BLOCKS_PER_STEP))

    @pl.when(f == n_steps - 1)
    def _():
        v = DEEPNORM_ALPHA * x_ref[...] + g_ref[...] * acc_s[...]
        o_ref[...] = _layernorm(v, lg_ref[...], lb_ref[...])


def _ffn(x1, ada, mod_spec, w_up, conv_w, conv_b, w_down, ln_g, ln_b, seq):
    tokens = x1.shape[0]
    tm, tf = FFN_TM, FFN_TF
    tiles_per_seq = seq // tm
    n_f = D_FF // tf
    n_steps = pl.cdiv(n_f, FFN_BLOCKS_PER_STEP)
    blocks_in_first = n_f - (n_steps - 1) * FFN_BLOCKS_PER_STEP
    halo_per_tile = tm // HALO
    n_halo_blocks = tokens // HALO
    row = pl.BlockSpec((1, D_MODEL), lambda i, f: (0, 0))

    block_specs, block_args = [], []
    for k in range(FFN_BLOCKS_PER_STEP):
        blk = lambda f, k=k: jnp.where((f == 0) & (k < blocks_in_first), k,
                                       blocks_in_first + jnp.maximum(f - 1, 0) * FFN_BLOCKS_PER_STEP + k)
        block_specs += [
            pl.BlockSpec((D_MODEL, tf), lambda i, f, blk=blk: (0, blk(f))),
            pl.BlockSpec((D_MODEL, tf), lambda i, f, blk=blk: (0, n_f + blk(f))),
            pl.BlockSpec((3, tf), lambda i, f, blk=blk: (0, blk(f))),
            pl.BlockSpec((3, tf), lambda i, f, blk=blk: (0, n_f + blk(f))),
            pl.BlockSpec((1, tf), lambda i, f, blk=blk: (0, blk(f))),
            pl.BlockSpec((1, tf), lambda i, f, blk=blk: (0, n_f + blk(f))),
            pl.BlockSpec((tf, D_MODEL), lambda i, f, blk=blk: (blk(f), 0)),
        ]
        block_args += [w_up, w_up, conv_w, conv_w, conv_b, conv_b, w_down]

    return pl.pallas_call(
        functools.partial(_ffn_kernel, tiles_per_seq=tiles_per_seq, n_steps=n_steps, blocks_in_first=blocks_in_first),
        out_shape=jax.ShapeDtypeStruct((tokens, D_MODEL), F32),
        grid=(tokens // tm, n_steps),
        in_specs=[
            pl.BlockSpec((tm, D_MODEL), lambda i, f: (i, 0)),
            pl.BlockSpec((HALO, D_MODEL), lambda i, f: (jnp.maximum(i * halo_per_tile - 1, 0), 0)),
            pl.BlockSpec((HALO, D_MODEL),
                         lambda i, f: (jnp.minimum((i + 1) * halo_per_tile, n_halo_blocks - 1), 0)),
            mod_spec(ADA_SC2, tiles_per_seq), mod_spec(ADA_SH2, tiles_per_seq), mod_spec(ADA_G2, tiles_per_seq),
        ] + block_specs + [row, row],
        out_specs=pl.BlockSpec((tm, D_MODEL), lambda i, f: (i, 0)),
        scratch_shapes=[
            pltpu.VMEM((tm + 2 * HALO, D_MODEL), BF16),
            pltpu.VMEM((tm, D_MODEL), F32),
        ],
        compiler_params=_params("arbitrary", "arbitrary"),
        name="ffn",
    )(x1, x1, x1, ada, ada, ada, *block_args, ln_g, ln_b)


def _rotary_tables(seq):
    half = D_HEAD // 2
    inv = 1.0 / (ROPE_BASE ** jnp.linspace(0.0, 1.0, half, dtype=F32))
    ang = jnp.arange(seq, dtype=F32)[:, None] * inv[None, :]
    return jnp.cos(ang), jnp.sin(ang)


def _retention_tables():
    hd = jnp.arange(N_HEADS, dtype=F32)
    lg = jnp.stack([jnp.log1p(-jnp.exp2(-RET_DECAY_EXP_FWD - hd)),
                    jnp.log1p(-jnp.exp2(-RET_DECAY_EXP_BWD - hd))], axis=1)
    pos = jnp.arange(CHUNK, dtype=F32)
    diff = pos[:, None] - pos[None, :]
    dmat = jnp.where(diff >= 0, jnp.exp(lg[:, :, None, None] * jnp.maximum(diff, 0.0)), 0.0)
    xi = jnp.exp(lg[:, :, None] * (pos + 1.0))
    zeta = jnp.exp(lg[:, :, None] * (CHUNK - 1.0 - pos))
    cd = jnp.exp(lg * CHUNK)
    dsum = dmat[:, 0] + dmat[:, 1, ::-1, ::-1]
    xi = jnp.stack([xi[:, 0], xi[:, 1, ::-1]], axis=1)
    zeta = jnp.stack([zeta[:, 0], zeta[:, 1, ::-1]], axis=1)
    xi = jnp.broadcast_to(xi[..., None], xi.shape + (D_HEAD,))
    zeta = zeta[:, :, None, :]
    cd = jnp.broadcast_to(cd[:, :, None, None], (N_HEADS, 2, 1, LANES))
    return dsum, xi, zeta, cd


def _split_pairs(w):
    w = w.reshape(D_MODEL, N_HEADS, D_HEAD // 2, 2)
    return jnp.concatenate([w[..., 0], w[..., 1]], axis=-1).reshape(D_MODEL, D_GROUP)


def _layout_w_in(w_in):
    sizes = [D_GROUP] * 4 + [2 * N_HEADS] * 2 + [D_GROUP] * 4
    mq, mk, mv, mo, mi, mf, rq, rk, rv, rg = jnp.split(w_in.astype(BF16), [int(s) for s in np.cumsum(sizes)[:-1]],
                                                       axis=1)
    w_main = jnp.concatenate([mq, mv, _split_pairs(rq), rv, mk, mo, _split_pairs(rk), rg], axis=1)
    w_gates = jnp.pad(jnp.concatenate([mi, mf], axis=1), ((0, 0), (0, LANES - 4 * N_HEADS)))
    return w_main, w_gates


def _trunk(x, ada, batch_offset, weights, tables, rotary):
    batch, seq, _ = x.shape
    (w_main, w_gates, bi_col, bf_col, norm_w, w_out, ln1_g, ln1_b, w_up, conv_w, conv_b, w_down,
     ln2_g, ln2_b) = weights
    x2 = x.reshape(batch * seq, D_MODEL)
    cos, sin = rotary
    n_chunks = seq // CHUNK

    def mod_spec(which, tiles_per_seq):
        return pl.BlockSpec((None, None, 1, D_MODEL),
                            lambda i, *_: (which, batch_offset + i // tiles_per_seq, 0, 0))

    za, zb, zk, gi, gf = _in_proj(x2, ada, mod_spec, w_main, w_gates, cos, sin, seq)
    pack, arow, wsrow, decrow = _gates(gi, gf, jnp.tile(bi_col, (n_chunks, 1)), jnp.tile(bf_col, (n_chunks, 1)),
                                       batch, seq)
    ym = _mlstm(za, zb, zk, pack, arow, wsrow, decrow, norm_w, batch, seq)
    yr = _retention(za, zb, zk, *tables, batch, seq)
    x1 = _out_proj(ym, yr, x2, ada, mod_spec, w_out, ln1_g, ln1_b, seq)
    out = _ffn(x1, ada, mod_spec, w_up, conv_w, conv_b, w_down, ln2_g, ln2_b, seq)
    return out.reshape(batch, seq, D_MODEL)


def _prepare_weights(w_in, b_igate, b_fgate, mlstm_norm_w, w_out, ln1_g, ln1_b, w_up, conv_w, conv_b, w_down,
                     ln2_g, ln2_b):
    bcast_col = lambda b: jnp.broadcast_to(b[:, None], (2 * N_HEADS, CHUNK))
    return _layout_w_in(w_in[0]) + (
        bcast_col(b_igate[0]), bcast_col(b_fgate[0]),
        mlstm_norm_w[0].reshape(N_HEADS, 1, D_HEAD), w_out[0].astype(BF16), ln1_g, ln1_b,
        w_up[0].astype(BF16), conv_w[0], conv_b, w_down[0].astype(BF16), ln2_g, ln2_b,
    )


def kernel(x_prompt, x_sample, c_prompt, c_sample, w_ada, b_ada, w_in, b_igate, b_fgate, mlstm_norm_w, w_out,
           ln1_g, ln1_b, w_up, conv_w, conv_b, w_down, ln2_g, ln2_b):
    assert w_ada.shape[0] == DEPTH
    n_prompt = c_prompt.shape[0]
    c_all = jnp.concatenate([c_prompt, c_sample], axis=0)
    pad_rows = -c_all.shape[0] % SUBLANES
    c_all = jnp.pad(c_all, ((0, pad_rows), (0, 0)))
    ada = _ada(c_all, w_ada[0], b_ada[0][None, :])[:, :, None, :]

    weights = _prepare_weights(w_in, b_igate, b_fgate, mlstm_norm_w, w_out, ln1_g, ln1_b, w_up, conv_w, conv_b,
                               w_down, ln2_g, ln2_b)
    tables = _retention_tables()
    rotary = _rotary_tables(max(x_prompt.shape[1], x_sample.shape[1]))
    y_prompt = _trunk(x_prompt, ada, 0, weights, tables, rotary)
    y_sample = _trunk(x_sample, ada, n_prompt, weights, tables, rotary)
    return (y_prompt, y_sample)
```
